```python
import jax, jax.numpy as jnp
from jax import lax
import numpy as np

D_MODEL = 2048
BATCH = 4
SEQ = 2048
DEPTH = 1

CONV_WIDTH = D_MODEL // 2
RET_WIDTH = D_MODEL - CONV_WIDTH
RET_HEADS = 4
RET_HEAD_DIM = RET_WIDTH // RET_HEADS
RET_CHUNK = 128
ROPE_BASE = 10000.0
CONV_KERNEL = 31
IN_PROJ_WIDTH = 2 * CONV_WIDTH + 4 * RET_WIDTH
N_EXPERTS = 32
TOP_K = 4
D_EXPERT = D_MODEL
SWIGLU_ALPHA = 1.702
SWIGLU_LIMIT = 7.0
MOE_BLOCK = 128
LN_EPS = 1e-5
DEEPNORM_ALPHA = float((2 * DEPTH) ** 0.25)
DEEPNORM_BETA = float((8 * DEPTH) ** -0.25)

kernel_name = "hymba_conformer_retnet_gptoss_deepnorm"


def _layer_norm(x, g, b):
    xf = x.astype(jnp.float32)
    mu = xf.mean(-1, keepdims=True)
    var = jnp.square(xf - mu).mean(-1, keepdims=True)
    return ((xf - mu) * lax.rsqrt(var + LN_EPS) * g + b).astype(x.dtype)


def _head_norm(r):
    mu = r.mean(-1, keepdims=True)
    var = jnp.square(r - mu).mean(-1, keepdims=True)
    return (r - mu) * lax.rsqrt(var + LN_EPS)


def _rotary(t):
    half = t.shape[-1] // 2
    inv_freq = ROPE_BASE ** (-jnp.arange(half, dtype=jnp.float32) / half)
    ang = jnp.arange(t.shape[1], dtype=jnp.float32)[:, None] * inv_freq[None, :]
    cos = jnp.cos(ang)[None, :, None, :]
    sin = jnp.sin(ang)[None, :, None, :]
    t = t.astype(jnp.float32)
    t1, t2 = t[..., :half], t[..., half:]
    return jnp.concatenate([t1 * cos - t2 * sin, t1 * sin + t2 * cos], axis=-1)


def _retention(q, k, v):
    bsz, s, h, dh = q.shape
    n_chunks = s // RET_CHUNK
    log_g = jnp.log(1.0 - 2.0 ** (-5.0 - jnp.arange(h, dtype=jnp.float32)))
    idx = jnp.arange(RET_CHUNK, dtype=jnp.float32)
    rel = idx[:, None] - idx[None, :]
    inner_decay = jnp.where(rel >= 0, jnp.exp(log_g[:, None, None] * jnp.maximum(rel, 0.0)), 0.0)
    q_decay = jnp.exp(log_g[:, None] * (idx[None, :] + 1.0))[None, :, :, None]
    k_decay = jnp.exp(log_g[:, None] * (RET_CHUNK - 1.0 - idx[None, :]))[None, :, :, None]
    chunk_decay = jnp.exp(log_g * RET_CHUNK)[None, :, None, None]

    def to_chunks(t):
        return t.reshape(bsz, n_chunks, RET_CHUNK, h, dh).transpose(1, 0, 3, 2, 4)

    def step(state, qkv):
        qi, ki, vi = qkv
        scores = jnp.einsum('bhcd,bhmd->bhcm', qi, ki) * inner_decay
        inner = jnp.einsum('bhcm,bhme->bhce', scores, vi)
        cross = jnp.einsum('bhcd,bhde->bhce', qi, state) * q_decay
        new_state = chunk_decay * state + jnp.einsum('bhmd,bhme->bhde', ki * k_decay, vi)
        return new_state, inner + cross

    state0 = jnp.zeros((bsz, h, dh, dh), jnp.float32)
    _, out = lax.scan(step, state0, (to_chunks(q), to_chunks(k), to_chunks(v)))
    return out.transpose(1, 0, 3, 2, 4).reshape(bsz, s, h, dh)


def _causal_depthwise_conv(u, w, b):
    out = lax.conv_general_dilated(
        u, w[:, None, :].astype(u.dtype), window_strides=(1,),
        padding=((CONV_KERNEL - 1, 0),), dimension_numbers=('NWC', 'WIO', 'NWC'),
        feature_group_count=u.shape[-1])
    return out + b


def _hybrid_mixer(h, w_in, b_in, conv_w, conv_b, conv_ln_g, conv_ln_b, w_out, b_out):
    bsz, s, _ = h.shape
    proj = h @ w_in + b_in
    cuts = [CONV_WIDTH, 2 * CONV_WIDTH, 2 * CONV_WIDTH + RET_WIDTH,
            2 * CONV_WIDTH + 2 * RET_WIDTH, 2 * CONV_WIDTH + 3 * RET_WIDTH]
    c_val, c_gate, q, k, v, g = jnp.split(proj, cuts, axis=-1)
    u = c_val * jax.nn.sigmoid(c_gate)
    u = _causal_depthwise_conv(u, conv_w, conv_b)
    u = jax.nn.silu(_layer_norm(u, conv_ln_g, conv_ln_b))
    heads = lambda t: t.reshape(bsz, s, RET_HEADS, RET_HEAD_DIM)
    qh = _rotary(heads(q))
    kh = _rotary(heads(k)) * (RET_HEAD_DIM ** -0.5)
    vh = heads(v).astype(jnp.float32)
    r = _head_norm(_retention(qh, kh, vh)).reshape(bsz, s, RET_WIDTH).astype(h.dtype)
    r = jax.nn.silu(g) * r
    return jnp.concatenate([u, r], axis=-1) @ w_out + b_out


def _moe(h, w_router, b_router, w_gate_up, b_gate_up, w_down, b_down):
    bsz, s, d = h.shape
    n_tok = bsz * s
    tok = h.reshape(n_tok, d)
    logits = (tok @ w_router + b_router).astype(jnp.float32)
    top_val, top_idx = lax.top_k(logits, TOP_K)
    gates = jax.nn.softmax(top_val, axis=-1)
    n_assign = n_tok * TOP_K
    flat_e = top_idx.reshape(n_assign)
    flat_tok = jnp.repeat(jnp.arange(n_tok, dtype=jnp.int32), TOP_K)
    flat_gate = gates.reshape(n_assign)
    order = jnp.argsort(flat_e)
    sorted_e = flat_e[order]
    counts = jnp.bincount(flat_e, length=N_EXPERTS)
    padded = ((counts + MOE_BLOCK - 1) // MOE_BLOCK) * MOE_BLOCK
    pad_end = jnp.cumsum(padded)
    pad_start = pad_end - padded
    grp_start = jnp.cumsum(counts) - counts
    dest = pad_start[sorted_e] + (jnp.arange(n_assign) - grp_start[sorted_e])
    n_rows = n_assign + N_EXPERTS * MOE_BLOCK
    n_blocks = n_rows // MOE_BLOCK
    row_tok = jnp.full((n_rows,), n_tok, jnp.int32).at[dest].set(flat_tok[order])
    row_gate = jnp.zeros((n_rows,), jnp.float32).at[dest].set(flat_gate[order])
    block_e = jnp.minimum(
        jnp.searchsorted(pad_end, jnp.arange(n_blocks) * MOE_BLOCK, side='right'), N_EXPERTS - 1)
    tok_pad = jnp.concatenate([tok, jnp.zeros((1, d), tok.dtype)], axis=0)
    xb = tok_pad[row_tok].reshape(n_blocks, MOE_BLOCK, d)

    def expert_block(args):
        xblk, e = args
        hgu = xblk @ w_gate_up[e] + b_gate_up[e]
        x_glu = jnp.minimum(hgu[:, :D_EXPERT], SWIGLU_LIMIT)
        x_lin = jnp.clip(hgu[:, D_EXPERT:], -SWIGLU_LIMIT, SWIGLU_LIMIT)
        act = x_glu * jax.nn.sigmoid(SWIGLU_ALPHA * x_glu) * (x_lin + 1.0)
        return act @ w_down[e] + b_down[e]

    yb = lax.map(expert_block, (xb, block_e)).reshape(n_rows, d)
    out = jnp.zeros((n_tok + 1, d), jnp.float32).at[row_tok].add(yb.astype(jnp.float32) * row_gate[:, None])
    return out[:n_tok].reshape(bsz, s, d).astype(h.dtype)


def setup_inputs(seed: int = 0) -> dict:
    key = jax.random.key(seed)
    ks = jax.random.split(key, 20)
    L, D, E, F = DEPTH, D_MODEL, N_EXPERTS, D_EXPERT
    nrm = lambda k, shape, scale: jax.random.normal(k, shape, jnp.float32) * scale
    return {
        "x": nrm(ks[0], (BATCH, SEQ, D), 1.0),
        "w_in": nrm(ks[1], (L, D, IN_PROJ_WIDTH), D ** -0.5),
        "b_in": nrm(ks[2], (L, IN_PROJ_WIDTH), 0.01),
        "conv_w": nrm(ks[3], (L, CONV_KERNEL, CONV_WIDTH), CONV_KERNEL ** -0.5),
        "conv_b": nrm(ks[4], (L, CONV_WIDTH), 0.01),
        "conv_ln_g": 1.0 + nrm(ks[5], (L, CONV_WIDTH), 0.02),
        "conv_ln_b": nrm(ks[6], (L, CONV_WIDTH), 0.01),
        "w_out": nrm(ks[7], (L, D, D), DEEPNORM_BETA * D ** -0.5),
        "b_out": nrm(ks[8], (L, D), 0.01),
        "ln1_g": 1.0 + nrm(ks[9], (L, D), 0.02),
        "ln1_b": nrm(ks[10], (L, D), 0.01),
        "w_router": nrm(ks[11], (L, D, E), D ** -0.5),
        "b_router": nrm(ks[12], (L, E), 0.01),
        "w_gate_up": nrm(ks[13], (L, E, D, 2 * F), D ** -0.5),
        "b_gate_up": nrm(ks[14], (L, E, 2 * F), 0.01),
        "w_down": nrm(ks[15], (L, E, F, D), DEEPNORM_BETA * F ** -0.5),
        "b_down": nrm(ks[16], (L, E, D), 0.01),
        "ln2_g": 1.0 + nrm(ks[17], (L, D), 0.02),
        "ln2_b": nrm(ks[18], (L, D), 0.01),
    }


def reference(x, w_in, b_in, conv_w, conv_b, conv_ln_g, conv_ln_b, w_out, b_out, ln1_g, ln1_b,
              w_router, b_router, w_gate_up, b_gate_up, w_down, b_down, ln2_g, ln2_b):
    h = x
    for l in range(DEPTH):
        mix = _hybrid_mixer(h, w_in[l], b_in[l], conv_w[l], conv_b[l], conv_ln_g[l], conv_ln_b[l],
                            w_out[l], b_out[l])
        h = _layer_norm(DEEPNORM_ALPHA * h + mix, ln1_g[l], ln1_b[l])
        ffn = _moe(h, w_router[l], b_router[l], w_gate_up[l], b_gate_up[l], w_down[l], b_down[l])
        h = _layer_norm(DEEPNORM_ALPHA * h + ffn, ln2_g[l], ln2_b[l])
    return h
```

```python
import functools
import math

import jax
import jax.numpy as jnp
from jax import lax
from jax.experimental import pallas as pl
from jax.experimental.pallas import tpu as pltpu

F32 = jnp.float32
BF16 = jnp.bfloat16
I32 = jnp.int32

RET_HEADS = 4
CONV_KERNEL = 31
ROPE_BASE = 10000.0
N_EXPERTS = 32
TOP_K = 4
SWIGLU_ALPHA = 1.702
SWIGLU_LIMIT = 7.0
LN_EPS = 1e-5
LOG_GAMMA = tuple(math.log(1.0 - 2.0 ** (-5.0 - h)) for h in range(RET_HEADS))

LANES = 128
SUBLANES = 8
VMEM_LIMIT = 58 * 1024 * 1024

TM_IN = 1024
TN_IN = 512
TS_CONV = 256
HALO = 32
RC_CONV = 32
CH_RET = 256
TM_OUT = 512
PREFIX_BLK = 256
SLOT_ROWS = 1280
ROW_BLK = 128
TF_EXP = 256
TB_DISP = 512
TB_COMB = 256


def _params(n_axes):
    return pltpu.CompilerParams(
        dimension_semantics=("arbitrary",) * n_axes, vmem_limit_bytes=VMEM_LIMIT)


def _sigmoid(x):
    return 1.0 / (1.0 + jnp.exp(-x))


def _to_row_tiles(v):
    return v.astype(BF16).reshape(v.shape[0], v.shape[1] // LANES, LANES)


def _from_row_tiles(t):
    return t.reshape(t.shape[0], t.shape[1] * t.shape[2])


def _in_proj_body(x_ref, wa_ref, wb_ref, ba_ref, bb_ref, cos_ref, sin_ref,
                  u_ref, q_ref, k_ref, v_ref, g_ref, xb_ref, *, n_a, n_r, head_dim):
    j = pl.program_id(1)

    @pl.when(j == 0)
    def _():
        xb_ref[...] = x_ref[...].astype(BF16)

    xb = xb_ref[...]
    acc = jnp.dot(xb, wa_ref[...].astype(BF16), preferred_element_type=F32) + ba_ref[...]

    @pl.when(j < n_a)
    def _():
        gate = jnp.dot(xb, wb_ref[...].astype(BF16), preferred_element_type=F32) + bb_ref[...]
        u_ref[...] = (acc * _sigmoid(gate)).astype(BF16)

    def rotary(o_ref, scale):
        cos = cos_ref[...]
        sin = sin_ref[...]
        half = head_dim // 2
        for hh in range(TN_IN // head_dim):
            c0 = hh * head_dim
            t1 = acc[:, c0:c0 + half]
            t2 = acc[:, c0 + half:c0 + head_dim]
            o_ref[:, c0:c0 + half] = ((t1 * cos - t2 * sin) * scale).astype(BF16)
            o_ref[:, c0 + half:c0 + head_dim] = ((t1 * sin + t2 * cos) * scale).astype(BF16)

    @pl.when((j >= n_a) & (j < n_a + n_r))
    def _():
        rotary(q_ref, 1.0)

    @pl.when((j >= n_a + n_r) & (j < n_a + 2 * n_r))
    def _():
        rotary(k_ref, head_dim ** -0.5)

    @pl.when((j >= n_a + 2 * n_r) & (j < n_a + 3 * n_r))
    def _():
        v_ref[...] = acc.astype(BF16)

    @pl.when(j >= n_a + 3 * n_r)
    def _():
        g_ref[...] = (acc * _sigmoid(acc)).astype(BF16)


def _in_proj(x2, w_in, b_in, cos, sin, *, seq, conv_w, ret_w):
    n, d = x2.shape
    head_dim = ret_w // RET_HEADS
    n_a = conv_w // TN_IN
    n_r = ret_w // TN_IN
    n_j = n_a + 4 * n_r
    seq_tiles = seq // TM_IN
    b2 = b_in.reshape(1, -1)

    def wa_idx(i, j):
        return (0, jnp.where(j < n_a, j, j + n_a))

    def wb_idx(i, j):
        return (0, jnp.where(j < n_a, n_a + j, 2 * n_a - 1))

    def seg(lo, cnt):
        return lambda i, j: (i, jnp.clip(j - lo, 0, cnt - 1))

    out_a = jax.ShapeDtypeStruct((n, conv_w), BF16)
    out_r = jax.ShapeDtypeStruct((n, ret_w), BF16)
    return pl.pallas_call(
        functools.partial(_in_proj_body, n_a=n_a, n_r=n_r, head_dim=head_dim),
        grid=(n // TM_IN, n_j),
        in_specs=[
            pl.BlockSpec((TM_IN, d), lambda i, j: (i, 0)),
            pl.BlockSpec((d, TN_IN), wa_idx),
            pl.BlockSpec((d, TN_IN), wb_idx),
            pl.BlockSpec((1, TN_IN), wa_idx),
            pl.BlockSpec((1, TN_IN), wb_idx),
            pl.BlockSpec((TM_IN, head_dim // 2), lambda i, j: (i % seq_tiles, 0)),
            pl.BlockSpec((TM_IN, head_dim // 2), lambda i, j: (i % seq_tiles, 0)),
        ],
        out_specs=[
            pl.BlockSpec((TM_IN, TN_IN), seg(0, n_a)),
            pl.BlockSpec((TM_IN, TN_IN), seg(n_a, n_r)),
            pl.BlockSpec((TM_IN, TN_IN), seg(n_a + n_r, n_r)),
            pl.BlockSpec((TM_IN, TN_IN), seg(n_a + 2 * n_r, n_r)),
            pl.BlockSpec((TM_IN, TN_IN), seg(n_a + 3 * n_r, n_r)),
        ],
        out_shape=[out_a, out_r, out_r, out_r, out_r],
        scratch_shapes=[pltpu.VMEM((TM_IN, d), BF16)],
        compiler_params=_params(2),
        name="in_proj",
    )(x2, w_in, w_in, b2, b2, cos, sin)


def _conv_body(halo_ref, cur_ref, w_ref, cb_ref, g_ref, b_ref, a_ref, ext_ref, sh_ref):
    si = pl.program_id(1)
    width = cur_ref.shape[1]
    halo = jnp.where(si == 0, 0.0, halo_ref[...].astype(F32))
    ext_ref[0:HALO, :] = halo
    ext_ref[HALO:, :] = cur_ref[...].astype(F32)
    first = HALO - (CONV_KERNEL - 1)
    ext_rows = ext_ref.shape[0]
    for s in range(SUBLANES):
        sh_ref[s, 0:ext_rows - SUBLANES, :] = ext_ref[s:s + ext_rows - SUBLANES, :]
    sh_ref[0, ext_rows - SUBLANES:, :] = ext_ref[ext_rows - SUBLANES:, :]

    def chunk(c, carry):
        r0 = pl.multiple_of(c * RC_CONV, RC_CONV)
        acc = jnp.broadcast_to(cb_ref[...], (RC_CONV, width))
        for j in range(CONV_KERNEL):
            off = first + j
            rows = pl.ds(pl.multiple_of(r0 + (off - off % SUBLANES), SUBLANES), RC_CONV)
            acc = acc + w_ref[j:j + 1, :] * sh_ref[off % SUBLANES, rows, :]
        mu = jnp.mean(acc, axis=-1, keepdims=True)
        dev = acc - mu
        var = jnp.mean(dev * dev, axis=-1, keepdims=True)
        y = dev * lax.rsqrt(var + LN_EPS) * g_ref[...] + b_ref[...]
        a_ref[pl.ds(r0, RC_CONV), :] = (y * _sigmoid(y)).astype(BF16)
        return carry

    lax.fori_loop(0, TS_CONV // RC_CONV, chunk, 0)


def _conv_branch(u, conv_w, conv_b, ln_g, ln_b, *, batch, seq):
    n, width = u.shape
    tiles = seq // TS_CONV

    def halo_idx(b, si):
        return (jnp.maximum(b * (seq // HALO) + si * (TS_CONV // HALO) - 1, 0), 0)

    row = lambda b, si: (0, 0)
    return pl.pallas_call(
        _conv_body,
        grid=(batch, tiles),
        in_specs=[
            pl.BlockSpec((HALO, width), halo_idx),
            pl.BlockSpec((TS_CONV, width), lambda b, si: (b * tiles + si, 0)),
            pl.BlockSpec((CONV_KERNEL, width), row),
            pl.BlockSpec((1, width), row),
            pl.BlockSpec((1, width), row),
            pl.BlockSpec((1, width), row),
        ],
        out_specs=pl.BlockSpec((TS_CONV, width), lambda b, si: (b * tiles + si, 0)),
        out_shape=jax.ShapeDtypeStruct((n, width), BF16),
        scratch_shapes=[pltpu.VMEM((HALO + TS_CONV, width), F32),
                        pltpu.VMEM((SUBLANES, HALO + TS_CONV, width), F32)],
        compiler_params=_params(2),
        name="conv_branch",
    )(u, u, conv_w, conv_b.reshape(1, -1), ln_g.reshape(1, -1), ln_b.reshape(1, -1))


def _retention_body(q_ref, k_ref, v_ref, g_ref, r_ref, state_ref, mask_ref, *, head_dim):
    b = pl.program_id(0)
    c = pl.program_id(1)

    @pl.when((b == 0) & (c == 0))
    def _():
        rel = (lax.broadcasted_iota(I32, (CH_RET, CH_RET), 0)
               - lax.broadcasted_iota(I32, (CH_RET, CH_RET), 1)).astype(F32)
        for h in range(RET_HEADS):
            mask_ref[h] = jnp.where(rel >= 0, jnp.exp(LOG_GAMMA[h] * jnp.maximum(rel, 0.0)), 0.0)

    @pl.when(c == 0)
    def _():
        state_ref[...] = jnp.zeros_like(state_ref)

    idx = lax.broadcasted_iota(I32, (CH_RET, 1), 0).astype(F32)
    nt = (((1,), (1,)), ((), ()))
    tn = (((0,), (0,)), ((), ()))
    for h in range(RET_HEADS):
        sl = slice(h * head_dim, (h + 1) * head_dim)
        qh = q_ref[:, sl]
        kh = k_ref[:, sl]
        vh = v_ref[:, sl]
        scores = lax.dot_general(qh, kh, nt, preferred_element_type=F32) * mask_ref[h]
        inner = jnp.dot(scores.astype(BF16), vh, preferred_element_type=F32)
        st = state_ref[h]
        q_decay = jnp.exp(LOG_GAMMA[h] * (idx + 1.0))
        cross = jnp.dot(qh, st.astype(BF16), preferred_element_type=F32) * q_decay
        k_decay = jnp.exp(LOG_GAMMA[h] * (CH_RET - 1.0 - idx))
        k_dec = (kh.astype(F32) * k_decay).astype(BF16)
        kv = lax.dot_general(k_dec, vh, tn, preferred_element_type=F32)
        state_ref[h] = math.exp(LOG_GAMMA[h] * CH_RET) * st + kv
        o = inner + cross
        mu = jnp.mean(o, axis=-1, keepdims=True)
        dev = o - mu
        var = jnp.mean(dev * dev, axis=-1, keepdims=True)
        r_ref[:, sl] = (g_ref[:, sl].astype(F32) * (dev * lax.rsqrt(var + LN_EPS))).astype(BF16)


def _retention(q, k, v, g, *, batch, seq):
    n, width = q.shape
    head_dim = width // RET_HEADS
    chunks = seq // CH_RET
    spec = pl.BlockSpec((CH_RET, width), lambda b, c: (b * chunks + c, 0))
    return pl.pallas_call(
        functools.partial(_retention_body, head_dim=head_dim),
        grid=(batch, chunks),
        in_specs=[spec, spec, spec, spec],
        out_specs=spec,
        out_shape=jax.ShapeDtypeStruct((n, width), BF16),
        scratch_shapes=[pltpu.VMEM((RET_HEADS, head_dim, head_dim), F32),
                        pltpu.VMEM((RET_HEADS, CH_RET, CH_RET), F32)],
        compiler_params=_params(2),
        name="retention",
    )(q, k, v, g)


def _out_proj_body(a_ref, r_ref, wt_ref, wb_ref, bo_ref, x_ref, g_ref, b_ref, wr_ref, br_ref,
                   h_ref, hp_ref, lt_ref, *, alpha):
    mix = (jnp.dot(a_ref[...], wt_ref[...], preferred_element_type=F32)
           + jnp.dot(r_ref[...], wb_ref[...], preferred_element_type=F32) + bo_ref[...])
    y = alpha * x_ref[...] + mix
    mu = jnp.mean(y, axis=-1, keepdims=True)
    dev = y - mu
    var = jnp.mean(dev * dev, axis=-1, keepdims=True)
    h1 = dev * lax.rsqrt(var + LN_EPS) * g_ref[...] + b_ref[...]
    h_ref[...] = h1
    hp_ref[...] = _to_row_tiles(h1)
    h_hi = h1.astype(BF16)
    h_lo = (h1 - h_hi.astype(F32)).astype(BF16)
    wr = wr_ref[...]
    w_hi = wr.astype(BF16)
    w_lo = (wr - w_hi.astype(F32)).astype(BF16)
    logits = (jnp.dot(h_hi, w_hi, preferred_element_type=F32)
              + jnp.dot(h_hi, w_lo, preferred_element_type=F32)
              + jnp.dot(h_lo, w_hi, preferred_element_type=F32) + br_ref[...])
    lt_ref[...] = logits.T[:N_EXPERTS, :]


def _out_proj(a, r, w_out, b_out, x2, ln_g, ln_b, w_router, b_router, *, alpha):
    n, d = x2.shape
    cw = a.shape[1]
    rw = r.shape[1]
    w_bf = w_out.astype(BF16)
    wr_pad = jnp.pad(w_router, ((0, 0), (0, LANES - N_EXPERTS)))
    br_pad = jnp.pad(b_router, (0, LANES - N_EXPERTS)).reshape(1, LANES)
    const = lambda i: (0, 0)
    return pl.pallas_call(
        functools.partial(_out_proj_body, alpha=alpha),
        grid=(n // TM_OUT,),
        in_specs=[
            pl.BlockSpec((TM_OUT, cw), lambda i: (i, 0)),
            pl.BlockSpec((TM_OUT, rw), lambda i: (i, 0)),
            pl.BlockSpec((cw, d), const),
            pl.BlockSpec((rw, d), lambda i: (cw // rw, 0)),
            pl.BlockSpec((1, d), const),
            pl.BlockSpec((TM_OUT, d), lambda i: (i, 0)),
            pl.BlockSpec((1, d), const),
            pl.BlockSpec((1, d), const),
            pl.BlockSpec((d, LANES), const),
            pl.BlockSpec((1, LANES), const),
        ],
        out_specs=[
            pl.BlockSpec((TM_OUT, d), lambda i: (i, 0)),
            pl.BlockSpec((TM_OUT, d // LANES, LANES), lambda i: (i, 0, 0)),
            pl.BlockSpec((N_EXPERTS, TM_OUT), lambda i: (0, i)),
        ],
        out_shape=[jax.ShapeDtypeStruct((n, d), F32),
                   jax.ShapeDtypeStruct((n, d // LANES, LANES), BF16),
                   jax.ShapeDtypeStruct((N_EXPERTS, n), F32)],
        compiler_params=_params(1),
        name="out_proj",
    )(a, r, w_bf, w_bf, b_out.reshape(1, -1), x2, ln_g.reshape(1, -1), ln_b.reshape(1, -1),
      wr_pad, br_pad)


def _route_body(lt_ref, pos_ref, gate_ref, meta_ref, rank_ref):
    n_tok = lt_ref.shape[1]
    slot = float(SLOT_ROWS)
    l = lt_ref[...]
    eio = lax.broadcasted_iota(I32, (N_EXPERTS, n_tok), 0).astype(F32)
    vals, idxs = [], []
    for _ in range(TOP_K):
        m = jnp.max(l, axis=0, keepdims=True)
        idx = jnp.min(jnp.where(l == m, eio, float(N_EXPERTS)), axis=0, keepdims=True)
        vals.append(m)
        idxs.append(idx)
        l = jnp.where(eio == idx, -jnp.inf, l)
    exps = [jnp.exp(v - vals[0]) for v in vals]
    den = exps[0] + exps[1] + exps[2] + exps[3]
    for k in range(TOP_K):
        gate_ref[k:k + 1, :] = exps[k] / den

    chosen = jnp.zeros((N_EXPERTS, n_tok), F32)
    for k in range(TOP_K):
        chosen = jnp.where(eio == idxs[k], 1.0, chosen)
    tri = (lax.broadcasted_iota(I32, (PREFIX_BLK, PREFIX_BLK), 0)
           < lax.broadcasted_iota(I32, (PREFIX_BLK, PREFIX_BLK), 1)).astype(BF16)
    count = jnp.zeros((N_EXPERTS, 1), F32)
    for blk in range(n_tok // PREFIX_BLK):
        cb = chosen[:, blk * PREFIX_BLK:(blk + 1) * PREFIX_BLK]
        pre = jnp.dot(cb.astype(BF16), tri, preferred_element_type=F32)
        rank_ref[:, blk * PREFIX_BLK:(blk + 1) * PREFIX_BLK] = pre + count
        count = count + jnp.sum(cb, axis=1, keepdims=True)

    n_items = jnp.broadcast_to(jnp.floor((count + (slot - 1.0)) / slot), (N_EXPERTS, LANES))
    rio = lax.broadcasted_iota(I32, (N_EXPERTS, LANES), 0)
    inc = n_items
    shift = 1
    while shift < N_EXPERTS:
        inc = inc + jnp.where(rio >= shift, pltpu.roll(inc, shift, axis=0), 0.0)
        shift *= 2
    start = inc - n_items

    rank = rank_ref[...]
    rq = jnp.floor((rank + 0.5) / slot)
    dest = (start[:, :1] + rq) * slot + (rank - rq * slot)
    for k in range(TOP_K):
        pos_ref[k:k + 1, :] = jnp.sum(
            jnp.where(eio == idxs[k], dest, 0.0), axis=0, keepdims=True).astype(I32)

    wio = lax.broadcasted_iota(I32, (N_EXPERTS, LANES), 1).astype(F32)
    owns = (wio >= start) & (wio < start + n_items)
    rows = jnp.clip(count - (wio - start) * slot, 0.0, slot)
    item_e = jnp.sum(jnp.where(owns, rio.astype(F32), 0.0), axis=0, keepdims=True)
    item_rows = jnp.sum(jnp.where(owns, rows, 0.0), axis=0, keepdims=True)
    meta_ref[...] = jnp.zeros_like(meta_ref)
    meta_ref[0:1, :] = item_e.astype(I32)
    meta_ref[1:2, :] = item_rows.astype(I32)
    meta_ref[2:3, :] = inc[N_EXPERTS - 1:N_EXPERTS, :].astype(I32)


def _route(lt):
    n_tok = lt.shape[1]
    return pl.pallas_call(
        _route_body,
        out_shape=[jax.ShapeDtypeStruct((TOP_K, n_tok), I32),
                   jax.ShapeDtypeStruct((TOP_K, n_tok), F32),
                   jax.ShapeDtypeStruct((8, LANES), I32)],
        scratch_shapes=[pltpu.VMEM((N_EXPERTS, n_tok), F32)],
        compiler_params=pltpu.CompilerParams(vmem_limit_bytes=VMEM_LIMIT),
        name="route",
    )(lt)


def _dispatch_body(pos_ref, h_ref, xs_ref, sem):
    def row_copy(t, k):
        return pltpu.make_async_copy(h_ref.at[t], xs_ref.at[pos_ref[k, t]], sem)

    def issue(t, carry):
        for k in range(TOP_K):
            row_copy(t, k).start()
        return carry

    lax.fori_loop(0, TB_DISP, issue, 0, unroll=4)
    for _ in range(TOP_K):
        pltpu.make_async_copy(h_ref, xs_ref.at[pl.ds(0, TB_DISP)], sem).wait()


def _dispatch(pos, hp, n_rows):
    n_tok, subl, lanes = hp.shape
    return pl.pallas_call(
        _dispatch_body,
        grid=(n_tok // TB_DISP,),
        in_specs=[
            pl.BlockSpec((TOP_K, TB_DISP), lambda i: (0, i), memory_space=pltpu.SMEM),
            pl.BlockSpec((TB_DISP, subl, lanes), lambda i: (i, 0, 0)),
        ],
        out_specs=pl.BlockSpec(memory_space=pl.ANY),
        out_shape=jax.ShapeDtypeStruct((n_rows, subl, lanes), BF16),
        scratch_shapes=[pltpu.SemaphoreType.DMA],
        compiler_params=_params(1),
        name="dispatch",
    )(pos, hp)


def _expert_body(ie_ref, nr_ref, na_ref, xs_ref, wg_ref, wl_ref, bg_ref, bl_ref, wd_ref, bd_ref,
                 ys_ref, xb_ref, acc_ref, wgb_ref, wlb_ref, wdb_ref, *, n_f):
    w = pl.program_id(0)
    f = pl.program_id(1)

    @pl.when(w < na_ref[0])
    def _():
        n_rows = nr_ref[w]

        @pl.when(f == 0)
        def _():
            live = lax.broadcasted_iota(I32, xs_ref.shape, 0) < n_rows
            xb_ref[...] = _from_row_tiles(jnp.where(live, xs_ref[...], jnp.zeros((), BF16)))
            acc_ref[...] = jnp.broadcast_to(bd_ref[0], acc_ref.shape)

        wgb_ref[...] = wg_ref[0].astype(BF16)
        wlb_ref[...] = wl_ref[0].astype(BF16)
        wdb_ref[...] = wd_ref[0].astype(BF16)
        bg = bg_ref[0]
        bl = bl_ref[0]

        def block(b, carry):
            rows = pl.ds(pl.multiple_of(b * ROW_BLK, ROW_BLK), ROW_BLK)
            xblk = xb_ref[rows, :]
            hg = jnp.dot(xblk, wgb_ref[...], preferred_element_type=F32) + bg
            hl = jnp.dot(xblk, wlb_ref[...], preferred_element_type=F32) + bl
            x_glu = jnp.minimum(hg, SWIGLU_LIMIT)
            x_lin = jnp.clip(hl, -SWIGLU_LIMIT, SWIGLU_LIMIT)
            act = x_glu * _sigmoid(SWIGLU_ALPHA * x_glu) * (x_lin + 1.0)
            acc_ref[rows, :] += jnp.dot(act.astype(BF16), wdb_ref[...], preferred_element_type=F32)
            return carry

        lax.fori_loop(0, (n_rows + (ROW_BLK - 1)) // ROW_BLK, block, 0)

        @pl.when(f == n_f - 1)
        def _():
            ys_ref[...] = _to_row_tiles(acc_ref[...])


def _experts(item_e, item_rows, n_active, xs, w_gate_up, b_gate_up, w_down, b_down, *, w_max):
    n_e, d, f2 = w_gate_up.shape
    d_exp = f2 // 2
    n_f = d_exp // TF_EXP
    slot_block = (SLOT_ROWS,) + xs.shape[1:]

    def item(w, na):
        return jnp.minimum(w, na[0] - 1)

    def fcol(w, f, na):
        return jnp.where(w < na[0], f, n_f - 1)

    def slot_idx(w, f, ie, nr, na):
        return (item(w, na), 0, 0)

    grid_spec = pltpu.PrefetchScalarGridSpec(
        num_scalar_prefetch=3,
        grid=(w_max, n_f),
        in_specs=[
            pl.BlockSpec(slot_block, slot_idx),
            pl.BlockSpec((1, d, TF_EXP), lambda w, f, ie, nr, na: (ie[item(w, na)], 0, fcol(w, f, na))),
            pl.BlockSpec((1, d, TF_EXP),
                         lambda w, f, ie, nr, na: (ie[item(w, na)], 0, n_f + fcol(w, f, na))),
            pl.BlockSpec((1, 1, TF_EXP), lambda w, f, ie, nr, na: (ie[item(w, na)], 0, fcol(w, f, na))),
            pl.BlockSpec((1, 1, TF_EXP),
                         lambda w, f, ie, nr, na: (ie[item(w, na)], 0, n_f + fcol(w, f, na))),
            pl.BlockSpec((1, TF_EXP, d), lambda w, f, ie, nr, na: (ie[item(w, na)], fcol(w, f, na), 0)),
            pl.BlockSpec((1, 1, d), lambda w, f, ie, nr, na: (ie[item(w, na)], 0, 0)),
        ],
        out_specs=pl.BlockSpec(slot_block, slot_idx),
        scratch_shapes=[
            pltpu.VMEM((SLOT_ROWS, d), BF16),
            pltpu.VMEM((SLOT_ROWS, d), F32),
            pltpu.VMEM((d, TF_EXP), BF16),
            pltpu.VMEM((d, TF_EXP), BF16),
            pltpu.VMEM((TF_EXP, d), BF16),
        ],
    )
    bgu = b_gate_up.reshape(n_e, 1, f2)
    return pl.pallas_call(
        functools.partial(_expert_body, n_f=n_f),
        grid_spec=grid_spec,
        out_shape=jax.ShapeDtypeStruct(xs.shape, BF16),
        compiler_params=_params(2),
        name="experts",
    )(item_e, item_rows, n_active, xs, w_gate_up, w_gate_up, bgu, bgu, w_down,
      b_down.reshape(n_e, 1, d))


def _combine_body(pos_ref, gt_ref, h_ref, g_ref, b_ref, ys_ref, o_ref, buf_ref, sem, *, alpha):
    def issue(t, carry):
        for k in range(TOP_K):
            pltpu.make_async_copy(ys_ref.at[pos_ref[k, t]], buf_ref.at[k, t], sem).start()
        return carry

    lax.fori_loop(0, TB_COMB, issue, 0, unroll=4)
    for k in range(TOP_K):
        pltpu.make_async_copy(ys_ref.at[pl.ds(0, TB_COMB)], buf_ref.at[k], sem).wait()

    ffn = jnp.zeros(h_ref.shape, F32)
    for k in range(TOP_K):
        ffn = ffn + gt_ref[:, k:k + 1] * _from_row_tiles(buf_ref[k]).astype(F32)
    y = alpha * h_ref[...] + ffn
    mu = jnp.mean(y, axis=-1, keepdims=True)
    dev = y - mu
    var = jnp.mean(dev * dev, axis=-1, keepdims=True)
    o_ref[...] = dev * lax.rsqrt(var + LN_EPS) * g_ref[...] + b_ref[...]


def _combine(pos, gates_t, h1, ln_g, ln_b, ys, *, alpha):
    n_tok, d = h1.shape
    const = lambda i: (0, 0)
    return pl.pallas_call(
        functools.partial(_combine_body, alpha=alpha),
        grid=(n_tok // TB_COMB,),
        in_specs=[
            pl.BlockSpec((TOP_K, TB_COMB), lambda i: (0, i), memory_space=pltpu.SMEM),
            pl.BlockSpec((TB_COMB, TOP_K), lambda i: (i, 0)),
            pl.BlockSpec((TB_COMB, d), lambda i: (i, 0)),
            pl.BlockSpec((1, d), const),
            pl.BlockSpec((1, d), const),
            pl.BlockSpec(memory_space=pl.ANY),
        ],
        out_specs=pl.BlockSpec((TB_COMB, d), lambda i: (i, 0)),
        out_shape=jax.ShapeDtypeStruct((n_tok, d), F32),
        scratch_shapes=[pltpu.VMEM((TOP_K, TB_COMB) + ys.shape[1:], BF16), pltpu.SemaphoreType.DMA],
        compiler_params=_params(1),
        name="combine",
    )(pos, gates_t, h1, ln_g.reshape(1, -1), ln_b.reshape(1, -1), ys)


def _rope_tables(seq, half):
    inv_freq = ROPE_BASE ** (-jnp.arange(half, dtype=F32) / half)
    ang = jnp.arange(seq, dtype=F32)[:, None] * inv_freq[None, :]
    return jnp.cos(ang), jnp.sin(ang)


def _layer(h, p, *, batch, seq, alpha, cos, sin):
    n_tok, d = h.shape
    conv_w = p["conv_w"].shape[1]
    ret_w = d - conv_w
    u, q, k, v, g = _in_proj(h, p["w_in"], p["b_in"], cos, sin, seq=seq, conv_w=conv_w, ret_w=ret_w)
    a = _conv_branch(u, p["conv_w"], p["conv_b"], p["conv_ln_g"], p["conv_ln_b"], batch=batch, seq=seq)
    r = _retention(q, k, v, g, batch=batch, seq=seq)
    h1, h1_packed, logits_t = _out_proj(a, r, p["w_out"], p["b_out"], h, p["ln1_g"], p["ln1_b"],
                                        p["w_router"], p["b_router"], alpha=alpha)
    pos, gates, meta = _route(logits_t)
    w_max = N_EXPERTS + (n_tok * TOP_K) // SLOT_ROWS
    xs = _dispatch(pos, h1_packed, w_max * SLOT_ROWS)
    ys = _experts(meta[0, :w_max], meta[1, :w_max], meta[2, :1], xs,
                  p["w_gate_up"], p["b_gate_up"], p["w_down"], p["b_down"], w_max=w_max)
    return _combine(pos, gates.T, h1, p["ln2_g"], p["ln2_b"], ys, alpha=alpha)


def kernel(x, w_in, b_in, conv_w, conv_b, conv_ln_g, conv_ln_b, w_out, b_out, ln1_g, ln1_b,
           w_router, b_router, w_gate_up, b_gate_up, w_down, b_down, ln2_g, ln2_b):
    batch, seq, d = x.shape
    depth = w_in.shape[0]
    alpha = float((2 * depth) ** 0.25)
    head_dim = (d - conv_w.shape[2]) // RET_HEADS
    cos, sin = _rope_tables(seq, head_dim // 2)
    stacked = dict(w_in=w_in, b_in=b_in, conv_w=conv_w, conv_b=conv_b, conv_ln_g=conv_ln_g,
                   conv_ln_b=conv_ln_b, w_out=w_out, b_out=b_out, ln1_g=ln1_g, ln1_b=ln1_b,
                   w_router=w_router, b_router=b_router, w_gate_up=w_gate_up, b_gate_up=b_gate_up,
                   w_down=w_down, b_down=b_down, ln2_g=ln2_g, ln2_b=ln2_b)
    h = x.reshape(batch * seq, d)
    for layer in range(depth):
        p = {name: val[layer] for name, val in stacked.items()}
        h = _layer(h, p, batch=batch, seq=seq, alpha=alpha, cos=cos, sin=sin)
    return h.reshape(batch, seq, d)
```

```python
import functools
import math

import jax
import jax.numpy as jnp
from jax import lax
from jax.experimental import pallas as pl
from jax.experimental.pallas import tpu as pltpu

F32 = jnp.float32
BF16 = jnp.bfloat16
I32 = jnp.int32

RET_HEADS = 4
CONV_KERNEL = 31
ROPE_BASE = 10000.0
N_EXPERTS = 32
TOP_K = 4
SWIGLU_ALPHA = 1.702
SWIGLU_LIMIT = 7.0
LN_EPS = 1e-5
LOG_GAMMA = tuple(math.log(1.0 - 2.0 ** (-5.0 - h)) for h in range(RET_HEADS))

LANES = 128
SUBLANES = 8
VMEM_LIMIT = 58 * 1024 * 1024

TM_IN = 1024
TN_IN = 512
TS_CONV = 256
HALO = 32
RC_CONV = 32
CH_RET = 256
TM_OUT = 512
PREFIX_BLK = 256
SLOT_ROWS = 1280
ROW_BLK = 128
BIG_BLK = 512
TF_EXP = 256
TB_DISP = 512
TB_COMB = 256


def _params(n_axes):
    return pltpu.CompilerParams(
        dimension_semantics=("arbitrary",) * n_axes, vmem_limit_bytes=VMEM_LIMIT)


def _sigmoid(x):
    return 1.0 / (1.0 + jnp.exp(-x))


def _to_row_tiles(v):
    return v.astype(BF16).reshape(v.shape[0], v.shape[1] // LANES, LANES)


def _from_row_tiles(t):
    return t.reshape(t.shape[0], t.shape[1] * t.shape[2])


def _in_proj_body(x_ref, wa_ref, wb_ref, ba_ref, bb_ref, cos_ref, sin_ref,
                  u_ref, q_ref, k_ref, v_ref, g_ref, xb_ref, *, n_a, n_r, head_dim):
    j = pl.program_id(1)

    @pl.when(j == 0)
    def _():
        xb_ref[...] = x_ref[...].astype(BF16)

    xb = xb_ref[...]
    acc = jnp.dot(xb, wa_ref[...].astype(BF16), preferred_element_type=F32) + ba_ref[...]

    @pl.when(j < n_a)
    def _():
        gate = jnp.dot(xb, wb_ref[...].astype(BF16), preferred_element_type=F32) + bb_ref[...]
        u_ref[...] = (acc * _sigmoid(gate)).astype(BF16)

    def rotary(o_ref, scale):
        cos = cos_ref[...]
        sin = sin_ref[...]
        half = head_dim // 2
        for hh in range(TN_IN // head_dim):
            c0 = hh * head_dim
            t1 = acc[:, c0:c0 + half]
            t2 = acc[:, c0 + half:c0 + head_dim]
            o_ref[:, c0:c0 + half] = ((t1 * cos - t2 * sin) * scale).astype(BF16)
            o_ref[:, c0 + half:c0 + head_dim] = ((t1 * sin + t2 * cos) * scale).astype(BF16)

    @pl.when((j >= n_a) & (j < n_a + n_r))
    def _():
        rotary(q_ref, 1.0)

    @pl.when((j >= n_a + n_r) & (j < n_a + 2 * n_r))
    def _():
        rotary(k_ref, head_dim ** -0.5)

    @pl.when((j >= n_a + 2 * n_r) & (j < n_a + 3 * n_r))
    def _():
        v_ref[...] = acc.astype(BF16)

    @pl.when(j >= n_a + 3 * n_r)
    def _():
        g_ref[...] = (acc * _sigmoid(acc)).astype(BF16)


def _in_proj(x2, w_in, b_in, cos, sin, *, seq, conv_w, ret_w):
    n, d = x2.shape
    head_dim = ret_w // RET_HEADS
    n_a = conv_w // TN_IN
    n_r = ret_w // TN_IN
    n_j = n_a + 4 * n_r
    seq_tiles = seq // TM_IN
    b2 = b_in.reshape(1, -1)

    def wa_idx(i, j):
        return (0, jnp.where(j < n_a, j, j + n_a))

    def wb_idx(i, j):
        return (0, jnp.where(j < n_a, n_a + j, 2 * n_a - 1))

    def seg(lo, cnt):
        return lambda i, j: (i, jnp.clip(j - lo, 0, cnt - 1))

    out_a = jax.ShapeDtypeStruct((n, conv_w), BF16)
    out_r = jax.ShapeDtypeStruct((n, ret_w), BF16)
    return pl.pallas_call(
        functools.partial(_in_proj_body, n_a=n_a, n_r=n_r, head_dim=head_dim),
        grid=(n // TM_IN, n_j),
        in_specs=[
            pl.BlockSpec((TM_IN, d), lambda i, j: (i, 0)),
            pl.BlockSpec((d, TN_IN), wa_idx),
            pl.BlockSpec((d, TN_IN), wb_idx),
            pl.BlockSpec((1, TN_IN), wa_idx),
            pl.BlockSpec((1, TN_IN), wb_idx),
            pl.BlockSpec((TM_IN, head_dim // 2), lambda i, j: (i % seq_tiles, 0)),
            pl.BlockSpec((TM_IN, head_dim // 2), lambda i, j: (i % seq_tiles, 0)),
        ],
        out_specs=[
            pl.BlockSpec((TM_IN, TN_IN), seg(0, n_a)),
            pl.BlockSpec((TM_IN, TN_IN), seg(n_a, n_r)),
            pl.BlockSpec((TM_IN, TN_IN), seg(n_a + n_r, n_r)),
            pl.BlockSpec((TM_IN, TN_IN), seg(n_a + 2 * n_r, n_r)),
            pl.BlockSpec((TM_IN, TN_IN), seg(n_a + 3 * n_r, n_r)),
        ],
        out_shape=[out_a, out_r, out_r, out_r, out_r],
        scratch_shapes=[pltpu.VMEM((TM_IN, d), BF16)],
        compiler_params=_params(2),
        name="in_proj",
    )(x2, w_in, w_in, b2, b2, cos, sin)


def _conv_body(halo_ref, cur_ref, w_ref, cb_ref, g_ref, b_ref, a_ref, ext_ref, sh_ref):
    si = pl.program_id(1)
    width = cur_ref.shape[1]
    halo = jnp.where(si == 0, 0.0, halo_ref[...].astype(F32))
    ext_ref[0:HALO, :] = halo
    ext_ref[HALO:, :] = cur_ref[...].astype(F32)
    first = HALO - (CONV_KERNEL - 1)
    ext_rows = ext_ref.shape[0]
    for s in range(SUBLANES):
        sh_ref[s, 0:ext_rows - SUBLANES, :] = ext_ref[s:s + ext_rows - SUBLANES, :]
    sh_ref[0, ext_rows - SUBLANES:, :] = ext_ref[ext_rows - SUBLANES:, :]

    def chunk(c, carry):
        r0 = pl.multiple_of(c * RC_CONV, RC_CONV)
        acc = jnp.broadcast_to(cb_ref[...], (RC_CONV, width))
        for j in range(CONV_KERNEL):
            off = first + j
            rows = pl.ds(pl.multiple_of(r0 + (off - off % SUBLANES), SUBLANES), RC_CONV)
            acc = acc + w_ref[j:j + 1, :] * sh_ref[off % SUBLANES, rows, :]
        mu = jnp.mean(acc, axis=-1, keepdims=True)
        dev = acc - mu
        var = jnp.mean(dev * dev, axis=-1, keepdims=True)
        y = dev * lax.rsqrt(var + LN_EPS) * g_ref[...] + b_ref[...]
        a_ref[pl.ds(r0, RC_CONV), :] = (y * _sigmoid(y)).astype(BF16)
        return carry

    lax.fori_loop(0, TS_CONV // RC_CONV, chunk, 0)


def _conv_branch(u, conv_w, conv_b, ln_g, ln_b, *, batch, seq):
    n, width = u.shape
    tiles = seq // TS_CONV

    def halo_idx(b, si):
        return (jnp.maximum(b * (seq // HALO) + si * (TS_CONV // HALO) - 1, 0), 0)

    row = lambda b, si: (0, 0)
    return pl.pallas_call(
        _conv_body,
        grid=(batch, tiles),
        in_specs=[
            pl.BlockSpec((HALO, width), halo_idx),
            pl.BlockSpec((TS_CONV, width), lambda b, si: (b * tiles + si, 0)),
            pl.BlockSpec((CONV_KERNEL, width), row),
            pl.BlockSpec((1, width), row),
            pl.BlockSpec((1, width), row),
            pl.BlockSpec((1, width), row),
        ],
        out_specs=pl.BlockSpec((TS_CONV, width), lambda b, si: (b * tiles + si, 0)),
        out_shape=jax.ShapeDtypeStruct((n, width), BF16),
        scratch_shapes=[pltpu.VMEM((HALO + TS_CONV, width), F32),
                        pltpu.VMEM((SUBLANES, HALO + TS_CONV, width), F32)],
        compiler_params=_params(2),
        name="conv_branch",
    )(u, u, conv_w, conv_b.reshape(1, -1), ln_g.reshape(1, -1), ln_b.reshape(1, -1))


def _retention_body(q_ref, k_ref, v_ref, g_ref, r_ref, state_ref, mask_ref, *, head_dim):
    b = pl.program_id(0)
    c = pl.program_id(1)

    @pl.when((b == 0) & (c == 0))
    def _():
        rel = (lax.broadcasted_iota(I32, (CH_RET, CH_RET), 0)
               - lax.broadcasted_iota(I32, (CH_RET, CH_RET), 1)).astype(F32)
        for h in range(RET_HEADS):
            mask_ref[h] = jnp.where(rel >= 0, jnp.exp(LOG_GAMMA[h] * jnp.maximum(rel, 0.0)), 0.0)

    @pl.when(c == 0)
    def _():
        state_ref[...] = jnp.zeros_like(state_ref)

    idx = lax.broadcasted_iota(I32, (CH_RET, 1), 0).astype(F32)
    nt = (((1,), (1,)), ((), ()))
    tn = (((0,), (0,)), ((), ()))
    for h in range(RET_HEADS):
        sl = slice(h * head_dim, (h + 1) * head_dim)
        qh = q_ref[:, sl]
        kh = k_ref[:, sl]
        vh = v_ref[:, sl]
        scores = lax.dot_general(qh, kh, nt, preferred_element_type=F32) * mask_ref[h]
        inner = jnp.dot(scores.astype(BF16), vh, preferred_element_type=F32)
        st = state_ref[h]
        q_decay = jnp.exp(LOG_GAMMA[h] * (idx + 1.0))
        cross = jnp.dot(qh, st.astype(BF16), preferred_element_type=F32) * q_decay
        k_decay = jnp.exp(LOG_GAMMA[h] * (CH_RET - 1.0 - idx))
        k_dec = (kh.astype(F32) * k_decay).astype(BF16)
        kv = lax.dot_general(k_dec, vh, tn, preferred_element_type=F32)
        state_ref[h] = math.exp(LOG_GAMMA[h] * CH_RET) * st + kv
        o = inner + cross
        mu = jnp.mean(o, axis=-1, keepdims=True)
        dev = o - mu
        var = jnp.mean(dev * dev, axis=-1, keepdims=True)
        r_ref[:, sl] = (g_ref[:, sl].astype(F32) * (dev * lax.rsqrt(var + LN_EPS))).astype(BF16)


def _retention(q, k, v, g, *, batch, seq):
    n, width = q.shape
    head_dim = width // RET_HEADS
    chunks = seq // CH_RET
    spec = pl.BlockSpec((CH_RET, width), lambda b, c: (b * chunks + c, 0))
    return pl.pallas_call(
        functools.partial(_retention_body, head_dim=head_dim),
        grid=(batch, chunks),
        in_specs=[spec, spec, spec, spec],
        out_specs=spec,
        out_shape=jax.ShapeDtypeStruct((n, width), BF16),
        scratch_shapes=[pltpu.VMEM((RET_HEADS, head_dim, head_dim), F32),
                        pltpu.VMEM((RET_HEADS, CH_RET, CH_RET), F32)],
        compiler_params=_params(2),
        name="retention",
    )(q, k, v, g)


def _out_proj_body(a_ref, r_ref, wt_ref, wb_ref, bo_ref, x_ref, g_ref, b_ref, wr_ref, br_ref,
                   h_ref, hp_ref, lt_ref, *, alpha):
    mix = (jnp.dot(a_ref[...], wt_ref[...], preferred_element_type=F32)
           + jnp.dot(r_ref[...], wb_ref[...], preferred_element_type=F32) + bo_ref[...])
    y = alpha * x_ref[...] + mix
    mu = jnp.mean(y, axis=-1, keepdims=True)
    dev = y - mu
    var = jnp.mean(dev * dev, axis=-1, keepdims=True)
    h1 = dev * lax.rsqrt(var + LN_EPS) * g_ref[...] + b_ref[...]
    h_ref[...] = h1
    hp_ref[...] = _to_row_tiles(h1)
    h_hi = h1.astype(BF16)
    h_lo = (h1 - h_hi.astype(F32)).astype(BF16)
    wr = wr_ref[...]
    w_hi = wr.astype(BF16)
    w_lo = (wr - w_hi.astype(F32)).astype(BF16)
    logits = (jnp.dot(h_hi, w_hi, preferred_element_type=F32)
              + jnp.dot(h_hi, w_lo, preferred_element_type=F32)
              + jnp.dot(h_lo, w_hi, preferred_element_type=F32) + br_ref[...])
    lt_ref[...] = logits.T[:N_EXPERTS, :]


def _out_proj(a, r, w_out, b_out, x2, ln_g, ln_b, w_router, b_router, *, alpha):
    n, d = x2.shape
    cw = a.shape[1]
    rw = r.shape[1]
    w_bf = w_out.astype(BF16)
    wr_pad = jnp.pad(w_router, ((0, 0), (0, LANES - N_EXPERTS)))
    br_pad = jnp.pad(b_router, (0, LANES - N_EXPERTS)).reshape(1, LANES)
    const = lambda i: (0, 0)
    return pl.pallas_call(
        functools.partial(_out_proj_body, alpha=alpha),
        grid=(n // TM_OUT,),
        in_specs=[
            pl.BlockSpec((TM_OUT, cw), lambda i: (i, 0)),
            pl.BlockSpec((TM_OUT, rw), lambda i: (i, 0)),
            pl.BlockSpec((cw, d), const),
            pl.BlockSpec((rw, d), lambda i: (cw // rw, 0)),
            pl.BlockSpec((1, d), const),
            pl.BlockSpec((TM_OUT, d), lambda i: (i, 0)),
            pl.BlockSpec((1, d), const),
            pl.BlockSpec((1, d), const),
            pl.BlockSpec((d, LANES), const),
            pl.BlockSpec((1, LANES), const),
        ],
        out_specs=[
            pl.BlockSpec((TM_OUT, d), lambda i: (i, 0)),
            pl.BlockSpec((TM_OUT, d // LANES, LANES), lambda i: (i, 0, 0)),
            pl.BlockSpec((N_EXPERTS, TM_OUT), lambda i: (0, i)),
        ],
        out_shape=[jax.ShapeDtypeStruct((n, d), F32),
                   jax.ShapeDtypeStruct((n, d // LANES, LANES), BF16),
                   jax.ShapeDtypeStruct((N_EXPERTS, n), F32)],
        compiler_params=_params(1),
        name="out_proj",
    )(a, r, w_bf, w_bf, b_out.reshape(1, -1), x2, ln_g.reshape(1, -1), ln_b.reshape(1, -1),
      wr_pad, br_pad)


def _route_body(lt_ref, pos_ref, gate_ref, meta_ref, rank_ref):
    n_tok = lt_ref.shape[1]
    slot = float(SLOT_ROWS)
    l = lt_ref[...]
    eio = lax.broadcasted_iota(I32, (N_EXPERTS, n_tok), 0).astype(F32)
    vals, idxs = [], []
    for _ in range(TOP_K):
        m = jnp.max(l, axis=0, keepdims=True)
        idx = jnp.min(jnp.where(l == m, eio, float(N_EXPERTS)), axis=0, keepdims=True)
        vals.append(m)
        idxs.append(idx)
        l = jnp.where(eio == idx, -jnp.inf, l)
    exps = [jnp.exp(v - vals[0]) for v in vals]
    den = exps[0] + exps[1] + exps[2] + exps[3]
    for k in range(TOP_K):
        gate_ref[k:k + 1, :] = exps[k] / den

    chosen = jnp.zeros((N_EXPERTS, n_tok), F32)
    for k in range(TOP_K):
        chosen = jnp.where(eio == idxs[k], 1.0, chosen)
    tri = (lax.broadcasted_iota(I32, (PREFIX_BLK, PREFIX_BLK), 0)
           < lax.broadcasted_iota(I32, (PREFIX_BLK, PREFIX_BLK), 1)).astype(BF16)
    count = jnp.zeros((N_EXPERTS, 1), F32)
    for blk in range(n_tok // PREFIX_BLK):
        cb = chosen[:, blk * PREFIX_BLK:(blk + 1) * PREFIX_BLK]
        pre = jnp.dot(cb.astype(BF16), tri, preferred_element_type=F32)
        rank_ref[:, blk * PREFIX_BLK:(blk + 1) * PREFIX_BLK] = pre + count
        count = count + jnp.sum(cb, axis=1, keepdims=True)

    n_items = jnp.broadcast_to(jnp.floor((count + (slot - 1.0)) / slot), (N_EXPERTS, LANES))
    rio = lax.broadcasted_iota(I32, (N_EXPERTS, LANES), 0)
    inc = n_items
    shift = 1
    while shift < N_EXPERTS:
        inc = inc + jnp.where(rio >= shift, pltpu.roll(inc, shift, axis=0), 0.0)
        shift *= 2
    start = inc - n_items

    rank = rank_ref[...]
    rq = jnp.floor((rank + 0.5) / slot)
    dest = (start[:, :1] + rq) * slot + (rank - rq * slot)
    for k in range(TOP_K):
        pos_ref[k:k + 1, :] = jnp.sum(
            jnp.where(eio == idxs[k], dest, 0.0), axis=0, keepdims=True).astype(I32)

    wio = lax.broadcasted_iota(I32, (N_EXPERTS, LANES), 1).astype(F32)
    owns = (wio >= start) & (wio < start + n_items)
    rows = jnp.clip(count - (wio - start) * slot, 0.0, slot)
    item_e = jnp.sum(jnp.where(owns, rio.astype(F32), 0.0), axis=0, keepdims=True)
    item_rows = jnp.sum(jnp.where(owns, rows, 0.0), axis=0, keepdims=True)
    meta_ref[...] = jnp.zeros_like(meta_ref)
    meta_ref[0:1, :] = item_e.astype(I32)
    meta_ref[1:2, :] = item_rows.astype(I32)
    meta_ref[2:3, :] = inc[N_EXPERTS - 1:N_EXPERTS, :].astype(I32)


def _route(lt):
    n_tok = lt.shape[1]
    return pl.pallas_call(
        _route_body,
        out_shape=[jax.ShapeDtypeStruct((TOP_K, n_tok), I32),
                   jax.ShapeDtypeStruct((TOP_K, n_tok), F32),
                   jax.ShapeDtypeStruct((8, LANES), I32)],
        scratch_shapes=[pltpu.VMEM((N_EXPERTS, n_tok), F32)],
        compiler_params=pltpu.CompilerParams(vmem_limit_bytes=VMEM_LIMIT),
        name="route",
    )(lt)


def _dispatch_body(pos_ref, h_ref, xs_ref, sem):
    def row_copy(t, k):
        return pltpu.make_async_copy(h_ref.at[t], xs_ref.at[pos_ref[k, t]], sem)

    def issue(t, carry):
        for k in range(TOP_K):
            row_copy(t, k).start(priority=k % 2)
        return carry

    lax.fori_loop(0, TB_DISP, issue, 0, unroll=4)
    for _ in range(TOP_K):
        pltpu.make_async_copy(h_ref, xs_ref.at[pl.ds(0, TB_DISP)], sem).wait()


def _dispatch(pos, hp, n_rows):
    n_tok, subl, lanes = hp.shape
    return pl.pallas_call(
        _dispatch_body,
        grid=(n_tok // TB_DISP,),
        in_specs=[
            pl.BlockSpec((TOP_K, TB_DISP), lambda i: (0, i), memory_space=pltpu.SMEM),
            pl.BlockSpec((TB_DISP, subl, lanes), lambda i: (i, 0, 0)),
        ],
        out_specs=pl.BlockSpec(memory_space=pl.ANY),
        out_shape=jax.ShapeDtypeStruct((n_rows, subl, lanes), BF16),
        scratch_shapes=[pltpu.SemaphoreType.DMA],
        compiler_params=_params(1),
        name="dispatch",
    )(pos, hp)


def _expert_body(ie_ref, nr_ref, na_ref, xs_ref, wg_ref, wl_ref, bg_ref, bl_ref, wd_ref, bd_ref,
                 ys_ref, xb_ref, acc_ref, wgb_ref, wlb_ref, wdb_ref, *, n_f):
    w = pl.program_id(0)
    f = pl.program_id(1)

    @pl.when(w < na_ref[0])
    def _():
        n_rows = nr_ref[w]

        @pl.when(f == 0)
        def _():
            live = lax.broadcasted_iota(I32, xs_ref.shape, 0) < n_rows
            xb_ref[...] = _from_row_tiles(jnp.where(live, xs_ref[...], jnp.zeros((), BF16)))
            acc_ref[...] = jnp.broadcast_to(bd_ref[0], acc_ref.shape)

        wgb_ref[...] = wg_ref[0].astype(BF16)
        wlb_ref[...] = wl_ref[0].astype(BF16)
        wdb_ref[...] = wd_ref[0].astype(BF16)
        bg = bg_ref[0]
        bl = bl_ref[0]

        def run_blocks(first_row, blk_rows, count):
            def block(b, carry):
                rows = pl.ds(pl.multiple_of(first_row + b * blk_rows, ROW_BLK), blk_rows)
                xblk = xb_ref[rows, :]
                hg = jnp.dot(xblk, wgb_ref[...], preferred_element_type=F32) + bg
                hl = jnp.dot(xblk, wlb_ref[...], preferred_element_type=F32) + bl
                x_glu = jnp.minimum(hg, SWIGLU_LIMIT)
                x_lin = jnp.clip(hl, -SWIGLU_LIMIT, SWIGLU_LIMIT)
                act = x_glu * _sigmoid(SWIGLU_ALPHA * x_glu) * (x_lin + 1.0)
                acc_ref[rows, :] += jnp.dot(act.astype(BF16), wdb_ref[...], preferred_element_type=F32)
                return carry

            lax.fori_loop(0, count, block, 0)

        n_big = n_rows // BIG_BLK
        tail_rows = n_rows - n_big * BIG_BLK
        run_blocks(0, BIG_BLK, n_big)
        run_blocks(n_big * BIG_BLK, ROW_BLK, (tail_rows + (ROW_BLK - 1)) // ROW_BLK)

        @pl.when(f == n_f - 1)
        def _():
            ys_ref[...] = _to_row_tiles(acc_ref[...])


def _experts(item_e, item_rows, n_active, xs, w_gate_up, b_gate_up, w_down, b_down, *, w_max):
    n_e, d, f2 = w_gate_up.shape
    d_exp = f2 // 2
    n_f = d_exp // TF_EXP
    slot_block = (SLOT_ROWS,) + xs.shape[1:]

    def item(w, na):
        return jnp.minimum(w, na[0] - 1)

    def fcol(w, f, na):
        return jnp.where(w < na[0], f, n_f - 1)

    def slot_idx(w, f, ie, nr, na):
        return (item(w, na), 0, 0)

    grid_spec = pltpu.PrefetchScalarGridSpec(
        num_scalar_prefetch=3,
        grid=(w_max, n_f),
        in_specs=[
            pl.BlockSpec(slot_block, slot_idx),
            pl.BlockSpec((1, d, TF_EXP), lambda w, f, ie, nr, na: (ie[item(w, na)], 0, fcol(w, f, na))),
            pl.BlockSpec((1, d, TF_EXP),
                         lambda w, f, ie, nr, na: (ie[item(w, na)], 0, n_f + fcol(w, f, na))),
            pl.BlockSpec((1, 1, TF_EXP), lambda w, f, ie, nr, na: (ie[item(w, na)], 0, fcol(w, f, na))),
            pl.BlockSpec((1, 1, TF_EXP),
                         lambda w, f, ie, nr, na: (ie[item(w, na)], 0, n_f + fcol(w, f, na))),
            pl.BlockSpec((1, TF_EXP, d), lambda w, f, ie, nr, na: (ie[item(w, na)], fcol(w, f, na), 0)),
            pl.BlockSpec((1, 1, d), lambda w, f, ie, nr, na: (ie[item(w, na)], 0, 0)),
        ],
        out_specs=pl.BlockSpec(slot_block, slot_idx),
        scratch_shapes=[
            pltpu.VMEM((SLOT_ROWS, d), BF16),
            pltpu.VMEM((SLOT_ROWS, d), F32),
            pltpu.VMEM((d, TF_EXP), BF16),
            pltpu.VMEM((d, TF_EXP), BF16),
            pltpu.VMEM((TF_EXP, d), BF16),
        ],
    )
    bgu = b_gate_up.reshape(n_e, 1, f2)
    return pl.pallas_call(
        functools.partial(_expert_body, n_f=n_f),
        grid_spec=grid_spec,
        out_shape=jax.ShapeDtypeStruct(xs.shape, BF16),
        compiler_params=_params(2),
        name="experts",
    )(item_e, item_rows, n_active, xs, w_gate_up, w_gate_up, bgu, bgu, w_down,
      b_down.reshape(n_e, 1, d))


def _combine_body(pos_ref, pos_next_ref, gt_ref, h_ref, g_ref, b_ref, ys_ref, o_ref, buf_ref, sem,
                  *, alpha):
    i = pl.program_id(0)
    slot = i % 2

    def gather(p_ref, s):
        def issue(t, carry):
            for k in range(TOP_K):
                pltpu.make_async_copy(
                    ys_ref.at[p_ref[k, t]], buf_ref.at[s, k, t], sem.at[s]).start(priority=k % 2)
            return carry

        lax.fori_loop(0, TB_COMB, issue, 0, unroll=4)

    @pl.when(i == 0)
    def _():
        gather(pos_ref, 0)

    @pl.when(i + 1 < pl.num_programs(0))
    def _():
        gather(pos_next_ref, 1 - slot)

    for k in range(TOP_K):
        pltpu.make_async_copy(
            ys_ref.at[pl.ds(0, TB_COMB)], buf_ref.at[slot, k], sem.at[slot]).wait()

    ffn = jnp.zeros(h_ref.shape, F32)
    for k in range(TOP_K):
        ffn = ffn + gt_ref[:, k:k + 1] * _from_row_tiles(buf_ref[slot, k]).astype(F32)
    y = alpha * h_ref[...] + ffn
    mu = jnp.mean(y, axis=-1, keepdims=True)
    dev = y - mu
    var = jnp.mean(dev * dev, axis=-1, keepdims=True)
    o_ref[...] = dev * lax.rsqrt(var + LN_EPS) * g_ref[...] + b_ref[...]


def _combine(pos, gates_t, h1, ln_g, ln_b, ys, *, alpha):
    n_tok, d = h1.shape
    const = lambda i: (0, 0)
    n_tiles = n_tok // TB_COMB
    return pl.pallas_call(
        functools.partial(_combine_body, alpha=alpha),
        grid=(n_tiles,),
        in_specs=[
            pl.BlockSpec((TOP_K, TB_COMB), lambda i: (0, i), memory_space=pltpu.SMEM),
            pl.BlockSpec((TOP_K, TB_COMB), lambda i: (0, jnp.minimum(i + 1, n_tiles - 1)),
                         memory_space=pltpu.SMEM),
            pl.BlockSpec((TB_COMB, TOP_K), lambda i: (i, 0)),
            pl.BlockSpec((TB_COMB, d), lambda i: (i, 0)),
            pl.BlockSpec((1, d), const),
            pl.BlockSpec((1, d), const),
            pl.BlockSpec(memory_space=pl.ANY),
        ],
        out_specs=pl.BlockSpec((TB_COMB, d), lambda i: (i, 0)),
        out_shape=jax.ShapeDtypeStruct((n_tok, d), F32),
        scratch_shapes=[pltpu.VMEM((2, TOP_K, TB_COMB) + ys.shape[1:], BF16),
                        pltpu.SemaphoreType.DMA((2,))],
        compiler_params=_params(1),
        name="combine",
    )(pos, pos, gates_t, h1, ln_g.reshape(1, -1), ln_b.reshape(1, -1), ys)


def _rope_tables(seq, half):
    inv_freq = ROPE_BASE ** (-jnp.arange(half, dtype=F32) / half)
    ang = jnp.arange(seq, dtype=F32)[:, None] * inv_freq[None, :]
    return jnp.cos(ang), jnp.sin(ang)


def _layer(h, p, *, batch, seq, alpha, cos, sin):
    n_tok, d = h.shape
    conv_w = p["conv_w"].shape[1]
    ret_w = d - conv_w
    u, q, k, v, g = _in_proj(h, p["w_in"], p["b_in"], cos, sin, seq=seq, conv_w=conv_w, ret_w=ret_w)
    a = _conv_branch(u, p["conv_w"], p["conv_b"], p["conv_ln_g"], p["conv_ln_b"], batch=batch, seq=seq)
    r = _retention(q, k, v, g, batch=batch, seq=seq)
    h1, h1_packed, logits_t = _out_proj(a, r, p["w_out"], p["b_out"], h, p["ln1_g"], p["ln1_b"],
                                        p["w_router"], p["b_router"], alpha=alpha)
    pos, gates, meta = _route(logits_t)
    w_max = N_EXPERTS + (n_tok * TOP_K) // SLOT_ROWS
    xs = _dispatch(pos, h1_packed, w_max * SLOT_ROWS)
    ys = _experts(meta[0, :w_max], meta[1, :w_max], meta[2, :1], xs,
                  p["w_gate_up"], p["b_gate_up"], p["w_down"], p["b_down"], w_max=w_max)
    return _combine(pos, gates.T, h1, p["ln2_g"], p["ln2_b"], ys, alpha=alpha)


def kernel(x, w_in, b_in, conv_w, conv_b, conv_ln_g, conv_ln_b, w_out, b_out, ln1_g, ln1_b,
           w_router, b_router, w_gate_up, b_gate_up, w_down, b_down, ln2_g, ln2_b):
    batch, seq, d = x.shape
    depth = w_in.shape[0]
    alpha = float((2 * depth) ** 0.25)
    head_dim = (d - conv_w.shape[2]) // RET_HEADS
    cos, sin = _rope_tables(seq, head_dim // 2)
    stacked = dict(w_in=w_in, b_in=b_in, conv_w=conv_w, conv_b=conv_b, conv_ln_g=conv_ln_g,
                   conv_ln_b=conv_ln_b, w_out=w_out, b_out=b_out, ln1_g=ln1_g, ln1_b=ln1_b,
                   w_router=w_router, b_router=b_router, w_gate_up=w_gate_up, b_gate_up=b_gate_up,
                   w_down=w_down, b_down=b_down, ln2_g=ln2_g, ln2_b=ln2_b)
    h = x.reshape(batch * seq, d)
    for layer in range(depth):
        p = {name: val[layer] for name, val in stacked.items()}
        h = _layer(h, p, batch=batch, seq=seq, alpha=alpha, cos=cos, sin=sin)
    return h.reshape(batch, seq, d)
```

```python
import functools
import math

import jax
import jax.numpy as jnp
from jax import lax
from jax.experimental import pallas as pl
from jax.experimental.pallas import tpu as pltpu

F32 = jnp.float32
BF16 = jnp.bfloat16
I32 = jnp.int32

RET_HEADS = 4
CONV_KERNEL = 31
ROPE_BASE = 10000.0
N_EXPERTS = 32
TOP_K = 4
SWIGLU_ALPHA = 1.702
SWIGLU_LIMIT = 7.0
LN_EPS = 1e-5
LOG_GAMMA = tuple(math.log(1.0 - 2.0 ** (-5.0 - h)) for h in range(RET_HEADS))

LANES = 128
SUBLANES = 8
VMEM_LIMIT = 58 * 1024 * 1024

TM_IN = 1024
TN_IN = 512
TS_CONV = 256
HALO = 32
RC_CONV = 32
CH_RET = 256
TM_OUT = 512
PREFIX_BLK = 256
SLOT_ROWS = 1280
ROW_BLK = 128
TF_EXP = 256
TB_DISP = 512
TB_COMB = 256


def _params(n_axes):
    return pltpu.CompilerParams(
        dimension_semantics=("arbitrary",) * n_axes, vmem_limit_bytes=VMEM_LIMIT)


def _sigmoid(x):
    return 1.0 / (1.0 + jnp.exp(-x))


def _to_row_tiles(v):
    return v.astype(BF16).reshape(v.shape[0], v.shape[1] // LANES, LANES)


def _from_row_tiles(t):
    return t.reshape(t.shape[0], t.shape[1] * t.shape[2])


def _in_proj_body(x_ref, wa_ref, wb_ref, ba_ref, bb_ref, cos_ref, sin_ref,
                  u_ref, q_ref, k_ref, v_ref, g_ref, xb_ref, *, n_a, n_r, head_dim):
    j = pl.program_id(1)

    @pl.when(j == 0)
    def _():
        xb_ref[...] = x_ref[...].astype(BF16)

    xb = xb_ref[...]
    acc = jnp.dot(xb, wa_ref[...].astype(BF16), preferred_element_type=F32) + ba_ref[...]

    @pl.when(j < n_a)
    def _():
        gate = jnp.dot(xb, wb_ref[...].astype(BF16), preferred_element_type=F32) + bb_ref[...]
        u_ref[...] = (acc * _sigmoid(gate)).astype(BF16)

    def rotary(o_ref, scale):
        cos = cos_ref[...]
        sin = sin_ref[...]
        half = head_dim // 2
        for hh in range(TN_IN // head_dim):
            c0 = hh * head_dim
            t1 = acc[:, c0:c0 + half]
            t2 = acc[:, c0 + half:c0 + head_dim]
            o_ref[:, c0:c0 + half] = ((t1 * cos - t2 * sin) * scale).astype(BF16)
            o_ref[:, c0 + half:c0 + head_dim] = ((t1 * sin + t2 * cos) * scale).astype(BF16)

    @pl.when((j >= n_a) & (j < n_a + n_r))
    def _():
        rotary(q_ref, 1.0)

    @pl.when((j >= n_a + n_r) & (j < n_a + 2 * n_r))
    def _():
        rotary(k_ref, head_dim ** -0.5)

    @pl.when((j >= n_a + 2 * n_r) & (j < n_a + 3 * n_r))
    def _():
        v_ref[...] = acc.astype(BF16)

    @pl.when(j >= n_a + 3 * n_r)
    def _():
        g_ref[...] = (acc * _sigmoid(acc)).astype(BF16)


def _in_proj(x2, w_in, b_in, cos, sin, *, seq, conv_w, ret_w):
    n, d = x2.shape
    head_dim = ret_w // RET_HEADS
    n_a = conv_w // TN_IN
    n_r = ret_w // TN_IN
    n_j = n_a + 4 * n_r
    seq_tiles = seq // TM_IN
    b2 = b_in.reshape(1, -1)

    def wa_idx(i, j):
        return (0, jnp.where(j < n_a, j, j + n_a))

    def wb_idx(i, j):
        return (0, jnp.where(j < n_a, n_a + j, 2 * n_a - 1))

    def seg(lo, cnt):
        return lambda i, j: (i, jnp.clip(j - lo, 0, cnt - 1))

    out_a = jax.ShapeDtypeStruct((n, conv_w), BF16)
    out_r = jax.ShapeDtypeStruct((n, ret_w), BF16)
    return pl.pallas_call(
        functools.partial(_in_proj_body, n_a=n_a, n_r=n_r, head_dim=head_dim),
        grid=(n // TM_IN, n_j),
        in_specs=[
            pl.BlockSpec((TM_IN, d), lambda i, j: (i, 0)),
            pl.BlockSpec((d, TN_IN), wa_idx),
            pl.BlockSpec((d, TN_IN), wb_idx),
            pl.BlockSpec((1, TN_IN), wa_idx),
            pl.BlockSpec((1, TN_IN), wb_idx),
            pl.BlockSpec((TM_IN, head_dim // 2), lambda i, j: (i % seq_tiles, 0)),
            pl.BlockSpec((TM_IN, head_dim // 2), lambda i, j: (i % seq_tiles, 0)),
        ],
        out_specs=[
            pl.BlockSpec((TM_IN, TN_IN), seg(0, n_a)),
            pl.BlockSpec((TM_IN, TN_IN), seg(n_a, n_r)),
            pl.BlockSpec((TM_IN, TN_IN), seg(n_a + n_r, n_r)),
            pl.BlockSpec((TM_IN, TN_IN), seg(n_a + 2 * n_r, n_r)),
            pl.BlockSpec((TM_IN, TN_IN), seg(n_a + 3 * n_r, n_r)),
        ],
        out_shape=[out_a, out_r, out_r, out_r, out_r],
        scratch_shapes=[pltpu.VMEM((TM_IN, d), BF16)],
        compiler_params=_params(2),
        name="in_proj",
    )(x2, w_in, w_in, b2, b2, cos, sin)


def _conv_body(halo_ref, cur_ref, w_ref, cb_ref, g_ref, b_ref, a_ref, ext_ref, sh_ref):
    si = pl.program_id(1)
    width = cur_ref.shape[1]
    halo = jnp.where(si == 0, 0.0, halo_ref[...].astype(F32))
    ext_ref[0:HALO, :] = halo
    ext_ref[HALO:, :] = cur_ref[...].astype(F32)
    first = HALO - (CONV_KERNEL - 1)
    ext_rows = ext_ref.shape[0]
    for s in range(SUBLANES):
        sh_ref[s, 0:ext_rows - SUBLANES, :] = ext_ref[s:s + ext_rows - SUBLANES, :]
    sh_ref[0, ext_rows - SUBLANES:, :] = ext_ref[ext_rows - SUBLANES:, :]

    def chunk(c, carry):
        r0 = pl.multiple_of(c * RC_CONV, RC_CONV)
        acc = jnp.broadcast_to(cb_ref[...], (RC_CONV, width))
        for j in range(CONV_KERNEL):
            off = first + j
            rows = pl.ds(pl.multiple_of(r0 + (off - off % SUBLANES), SUBLANES), RC_CONV)
            acc = acc + w_ref[j:j + 1, :] * sh_ref[off % SUBLANES, rows, :]
        mu = jnp.mean(acc, axis=-1, keepdims=True)
        dev = acc - mu
        var = jnp.mean(dev * dev, axis=-1, keepdims=True)
        y = dev * lax.rsqrt(var + LN_EPS) * g_ref[...] + b_ref[...]
        a_ref[pl.ds(r0, RC_CONV), :] = (y * _sigmoid(y)).astype(BF16)
        return carry

    lax.fori_loop(0, TS_CONV // RC_CONV, chunk, 0)


def _conv_branch(u, conv_w, conv_b, ln_g, ln_b, *, batch, seq):
    n, width = u.shape
    tiles = seq // TS_CONV

    def halo_idx(b, si):
        return (jnp.maximum(b * (seq // HALO) + si * (TS_CONV // HALO) - 1, 0), 0)

    row = lambda b, si: (0, 0)
    return pl.pallas_call(
        _conv_body,
        grid=(batch, tiles),
        in_specs=[
            pl.BlockSpec((HALO, width), halo_idx),
            pl.BlockSpec((TS_CONV, width), lambda b, si: (b * tiles + si, 0)),
            pl.BlockSpec((CONV_KERNEL, width), row),
            pl.BlockSpec((1, width), row),
            pl.BlockSpec((1, width), row),
            pl.BlockSpec((1, width), row),
        ],
        out_specs=pl.BlockSpec((TS_CONV, width), lambda b, si: (b * tiles + si, 0)),
        out_shape=jax.ShapeDtypeStruct((n, width), BF16),
        scratch_shapes=[pltpu.VMEM((HALO + TS_CONV, width), F32),
                        pltpu.VMEM((SUBLANES, HALO + TS_CONV, width), F32)],
        compiler_params=_params(2),
        name="conv_branch",
    )(u, u, conv_w, conv_b.reshape(1, -1), ln_g.reshape(1, -1), ln_b.reshape(1, -1))


def _retention_body(q_ref, k_ref, v_ref, g_ref, r_ref, state_ref, mask_ref, *, head_dim):
    b = pl.program_id(0)
    c = pl.program_id(1)

    @pl.when((b == 0) & (c == 0))
    def _():
        rel = (lax.broadcasted_iota(I32, (CH_RET, CH_RET), 0)
               - lax.broadcasted_iota(I32, (CH_RET, CH_RET), 1)).astype(F32)
        for h in range(RET_HEADS):
            mask_ref[h] = jnp.where(rel >= 0, jnp.exp(LOG_GAMMA[h] * jnp.maximum(rel, 0.0)), 0.0)

    @pl.when(c == 0)
    def _():
        state_ref[...] = jnp.zeros_like(state_ref)

    idx = lax.broadcasted_iota(I32, (CH_RET, 1), 0).astype(F32)
    nt = (((1,), (1,)), ((), ()))
    tn = (((0,), (0,)), ((), ()))
    for h in range(RET_HEADS):
        sl = slice(h * head_dim, (h + 1) * head_dim)
        qh = q_ref[:, sl]
        kh = k_ref[:, sl]
        vh = v_ref[:, sl]
        scores = lax.dot_general(qh, kh, nt, preferred_element_type=F32) * mask_ref[h]
        inner = jnp.dot(scores.astype(BF16), vh, preferred_element_type=F32)
        st = state_ref[h]
        q_decay = jnp.exp(LOG_GAMMA[h] * (idx + 1.0))
        cross = jnp.dot(qh, st.astype(BF16), preferred_element_type=F32) * q_decay
        k_decay = jnp.exp(LOG_GAMMA[h] * (CH_RET - 1.0 - idx))
        k_dec = (kh.astype(F32) * k_decay).astype(BF16)
        kv = lax.dot_general(k_dec, vh, tn, preferred_element_type=F32)
        state_ref[h] = math.exp(LOG_GAMMA[h] * CH_RET) * st + kv
        o = inner + cross
        mu = jnp.mean(o, axis=-1, keepdims=True)
        dev = o - mu
        var = jnp.mean(dev * dev, axis=-1, keepdims=True)
        r_ref[:, sl] = (g_ref[:, sl].astype(F32) * (dev * lax.rsqrt(var + LN_EPS))).astype(BF16)


def _retention(q, k, v, g, *, batch, seq):
    n, width = q.shape
    head_dim = width // RET_HEADS
    chunks = seq // CH_RET
    spec = pl.BlockSpec((CH_RET, width), lambda b, c: (b * chunks + c, 0))
    return pl.pallas_call(
        functools.partial(_retention_body, head_dim=head_dim),
        grid=(batch, chunks),
        in_specs=[spec, spec, spec, spec],
        out_specs=spec,
        out_shape=jax.ShapeDtypeStruct((n, width), BF16),
        scratch_shapes=[pltpu.VMEM((RET_HEADS, head_dim, head_dim), F32),
                        pltpu.VMEM((RET_HEADS, CH_RET, CH_RET), F32)],
        compiler_params=_params(2),
        name="retention",
    )(q, k, v, g)


def _out_proj_body(a_ref, r_ref, wt_ref, wb_ref, bo_ref, x_ref, g_ref, b_ref, wr_ref, br_ref,
                   h_ref, hp_ref, lt_ref, *, alpha):
    mix = (jnp.dot(a_ref[...], wt_ref[...], preferred_element_type=F32)
           + jnp.dot(r_ref[...], wb_ref[...], preferred_element_type=F32) + bo_ref[...])
    y = alpha * x_ref[...] + mix
    mu = jnp.mean(y, axis=-1, keepdims=True)
    dev = y - mu
    var = jnp.mean(dev * dev, axis=-1, keepdims=True)
    h1 = dev * lax.rsqrt(var + LN_EPS) * g_ref[...] + b_ref[...]
    h_ref[...] = h1
    hp_ref[...] = _to_row_tiles(h1)
    h_hi = h1.astype(BF16)
    h_lo = (h1 - h_hi.astype(F32)).astype(BF16)
    wr = wr_ref[...]
    w_hi = wr.astype(BF16)
    w_lo = (wr - w_hi.astype(F32)).astype(BF16)
    logits = (jnp.dot(h_hi, w_hi, preferred_element_type=F32)
              + jnp.dot(h_hi, w_lo, preferred_element_type=F32)
              + jnp.dot(h_lo, w_hi, preferred_element_type=F32) + br_ref[...])
    lt_ref[...] = logits.T[:N_EXPERTS, :]


def _out_proj(a, r, w_out, b_out, x2, ln_g, ln_b, w_router, b_router, *, alpha):
    n, d = x2.shape
    cw = a.shape[1]
    rw = r.shape[1]
    w_bf = w_out.astype(BF16)
    wr_pad = jnp.pad(w_router, ((0, 0), (0, LANES - N_EXPERTS)))
    br_pad = jnp.pad(b_router, (0, LANES - N_EXPERTS)).reshape(1, LANES)
    const = lambda i: (0, 0)
    return pl.pallas_call(
        functools.partial(_out_proj_body, alpha=alpha),
        grid=(n // TM_OUT,),
        in_specs=[
            pl.BlockSpec((TM_OUT, cw), lambda i: (i, 0)),
            pl.BlockSpec((TM_OUT, rw), lambda i: (i, 0)),
            pl.BlockSpec((cw, d), const),
            pl.BlockSpec((rw, d), lambda i: (cw // rw, 0)),
            pl.BlockSpec((1, d), const),
            pl.BlockSpec((TM_OUT, d), lambda i: (i, 0)),
            pl.BlockSpec((1, d), const),
            pl.BlockSpec((1, d), const),
            pl.BlockSpec((d, LANES), const),
            pl.BlockSpec((1, LANES), const),
        ],
        out_specs=[
            pl.BlockSpec((TM_OUT, d), lambda i: (i, 0)),
            pl.BlockSpec((TM_OUT, d // LANES, LANES), lambda i: (i, 0, 0)),
            pl.BlockSpec((N_EXPERTS, TM_OUT), lambda i: (0, i)),
        ],
        out_shape=[jax.ShapeDtypeStruct((n, d), F32),
                   jax.ShapeDtypeStruct((n, d // LANES, LANES), BF16),
                   jax.ShapeDtypeStruct((N_EXPERTS, n), F32)],
        compiler_params=_params(1),
        name="out_proj",
    )(a, r, w_bf, w_bf, b_out.reshape(1, -1), x2, ln_g.reshape(1, -1), ln_b.reshape(1, -1),
      wr_pad, br_pad)


def _route_body(lt_ref, pos_ref, gate_ref, meta_ref, rank_ref):
    n_tok = lt_ref.shape[1]
    slot = float(SLOT_ROWS)
    l = lt_ref[...]
    eio = lax.broadcasted_iota(I32, (N_EXPERTS, n_tok), 0).astype(F32)
    vals, idxs = [], []
    for _ in range(TOP_K):
        m = jnp.max(l, axis=0, keepdims=True)
        idx = jnp.min(jnp.where(l == m, eio, float(N_EXPERTS)), axis=0, keepdims=True)
        vals.append(m)
        idxs.append(idx)
        l = jnp.where(eio == idx, -jnp.inf, l)
    exps = [jnp.exp(v - vals[0]) for v in vals]
    den = exps[0] + exps[1] + exps[2] + exps[3]
    for k in range(TOP_K):
        gate_ref[k:k + 1, :] = exps[k] / den

    chosen = jnp.zeros((N_EXPERTS, n_tok), F32)
    for k in range(TOP_K):
        chosen = jnp.where(eio == idxs[k], 1.0, chosen)
    tri = (lax.broadcasted_iota(I32, (PREFIX_BLK, PREFIX_BLK), 0)
           < lax.broadcasted_iota(I32, (PREFIX_BLK, PREFIX_BLK), 1)).astype(BF16)
    count = jnp.zeros((N_EXPERTS, 1), F32)
    for blk in range(n_tok // PREFIX_BLK):
        cb = chosen[:, blk * PREFIX_BLK:(blk + 1) * PREFIX_BLK]
        pre = jnp.dot(cb.astype(BF16), tri, preferred_element_type=F32)
        rank_ref[:, blk * PREFIX_BLK:(blk + 1) * PREFIX_BLK] = pre + count
        count = count + jnp.sum(cb, axis=1, keepdims=True)

    n_items = jnp.broadcast_to(jnp.floor((count + (slot - 1.0)) / slot), (N_EXPERTS, LANES))
    rio = lax.broadcasted_iota(I32, (N_EXPERTS, LANES), 0)
    inc = n_items
    shift = 1
    while shift < N_EXPERTS:
        inc = inc + jnp.where(rio >= shift, pltpu.roll(inc, shift, axis=0), 0.0)
        shift *= 2
    start = inc - n_items

    rank = rank_ref[...]
    rq = jnp.floor((rank + 0.5) / slot)
    dest = (start[:, :1] + rq) * slot + (rank - rq * slot)
    for k in range(TOP_K):
        pos_ref[k:k + 1, :] = jnp.sum(
            jnp.where(eio == idxs[k], dest, 0.0), axis=0, keepdims=True).astype(I32)

    wio = lax.broadcasted_iota(I32, (N_EXPERTS, LANES), 1).astype(F32)
    owns = (wio >= start) & (wio < start + n_items)
    rows = jnp.clip(count - (wio - start) * slot, 0.0, slot)
    item_e = jnp.sum(jnp.where(owns, rio.astype(F32), 0.0), axis=0, keepdims=True)
    item_rows = jnp.sum(jnp.where(owns, rows, 0.0), axis=0, keepdims=True)
    meta_ref[...] = jnp.zeros_like(meta_ref)
    meta_ref[0:1, :] = item_e.astype(I32)
    meta_ref[1:2, :] = item_rows.astype(I32)
    meta_ref[2:3, :] = inc[N_EXPERTS - 1:N_EXPERTS, :].astype(I32)


def _route(lt):
    n_tok = lt.shape[1]
    return pl.pallas_call(
        _route_body,
        out_shape=[jax.ShapeDtypeStruct((TOP_K, n_tok), I32),
                   jax.ShapeDtypeStruct((TOP_K, n_tok), F32),
                   jax.ShapeDtypeStruct((8, LANES), I32)],
        scratch_shapes=[pltpu.VMEM((N_EXPERTS, n_tok), F32)],
        compiler_params=pltpu.CompilerParams(vmem_limit_bytes=VMEM_LIMIT),
        name="route",
    )(lt)


def _dispatch_body(pos_ref, h_ref, xs_ref, sem):
    def row_copy(t, k):
        return pltpu.make_async_copy(h_ref.at[t], xs_ref.at[pos_ref[k, t]], sem)

    def issue(t, carry):
        for k in range(TOP_K):
            row_copy(t, k).start(priority=k % 2)
        return carry

    lax.fori_loop(0, TB_DISP, issue, 0, unroll=4)
    for _ in range(TOP_K):
        pltpu.make_async_copy(h_ref, xs_ref.at[pl.ds(0, TB_DISP)], sem).wait()


def _dispatch(pos, hp, n_rows):
    n_tok, subl, lanes = hp.shape
    return pl.pallas_call(
        _dispatch_body,
        grid=(n_tok // TB_DISP,),
        in_specs=[
            pl.BlockSpec((TOP_K, TB_DISP), lambda i: (0, i), memory_space=pltpu.SMEM),
            pl.BlockSpec((TB_DISP, subl, lanes), lambda i: (i, 0, 0)),
        ],
        out_specs=pl.BlockSpec(memory_space=pl.ANY),
        out_shape=jax.ShapeDtypeStruct((n_rows, subl, lanes), BF16),
        scratch_shapes=[pltpu.SemaphoreType.DMA],
        compiler_params=_params(1),
        name="dispatch",
    )(pos, hp)


def _expert_body(ie_ref, nr_ref, na_ref, xs_ref, wg_ref, wl_ref, bg_ref, bl_ref, wd_ref, bd_ref,
                 ys_ref, xb_ref, acc_ref, *, n_f):
    w = pl.program_id(0)
    f = pl.program_id(1)

    @pl.when(w < na_ref[0])
    def _():
        n_rows = nr_ref[w]

        @pl.when(f == 0)
        def _():
            live = lax.broadcasted_iota(I32, xs_ref.shape, 0) < n_rows
            xb_ref[...] = _from_row_tiles(jnp.where(live, xs_ref[...], jnp.zeros((), BF16)))
            acc_ref[...] = jnp.broadcast_to(bd_ref[0], acc_ref.shape)

        bg = bg_ref[0]
        bl = bl_ref[0]

        def mlp_rows(n):
            xblk = xb_ref[0:n, :]
            hg = jnp.dot(xblk, wg_ref[0].astype(BF16), preferred_element_type=F32) + bg
            hl = jnp.dot(xblk, wl_ref[0].astype(BF16), preferred_element_type=F32) + bl
            x_glu = jnp.minimum(hg, SWIGLU_LIMIT)
            x_lin = jnp.clip(hl, -SWIGLU_LIMIT, SWIGLU_LIMIT)
            act = x_glu * _sigmoid(SWIGLU_ALPHA * x_glu) * (x_lin + 1.0)
            acc_ref[0:n, :] += jnp.dot(
                act.astype(BF16), wd_ref[0].astype(BF16), preferred_element_type=F32)

        n_sub = (n_rows + (ROW_BLK - 1)) // ROW_BLK
        for m in range(1, SLOT_ROWS // ROW_BLK + 1):
            @pl.when(n_sub == m)
            def _(m=m):
                mlp_rows(m * ROW_BLK)

        @pl.when(f == n_f - 1)
        def _():
            ys_ref[...] = _to_row_tiles(acc_ref[...])


def _experts(item_e, item_rows, n_active, xs, w_gate_up, b_gate_up, w_down, b_down, *, w_max):
    n_e, d, f2 = w_gate_up.shape
    d_exp = f2 // 2
    n_f = d_exp // TF_EXP
    slot_block = (SLOT_ROWS,) + xs.shape[1:]

    def item(w, na):
        return jnp.minimum(w, na[0] - 1)

    def fcol(w, f, na):
        return jnp.where(w < na[0], f, n_f - 1)

    def slot_idx(w, f, ie, nr, na):
        return (item(w, na), 0, 0)

    grid_spec = pltpu.PrefetchScalarGridSpec(
        num_scalar_prefetch=3,
        grid=(w_max, n_f),
        in_specs=[
            pl.BlockSpec(slot_block, slot_idx),
            pl.BlockSpec((1, d, TF_EXP), lambda w, f, ie, nr, na: (ie[item(w, na)], 0, fcol(w, f, na))),
            pl.BlockSpec((1, d, TF_EXP),
                         lambda w, f, ie, nr, na: (ie[item(w, na)], 0, n_f + fcol(w, f, na))),
            pl.BlockSpec((1, 1, TF_EXP), lambda w, f, ie, nr, na: (ie[item(w, na)], 0, fcol(w, f, na))),
            pl.BlockSpec((1, 1, TF_EXP),
                         lambda w, f, ie, nr, na: (ie[item(w, na)], 0, n_f + fcol(w, f, na))),
            pl.BlockSpec((1, TF_EXP, d), lambda w, f, ie, nr, na: (ie[item(w, na)], fcol(w, f, na), 0)),
            pl.BlockSpec((1, 1, d), lambda w, f, ie, nr, na: (ie[item(w, na)], 0, 0)),
        ],
        out_specs=pl.BlockSpec(slot_block, slot_idx),
        scratch_shapes=[
            pltpu.VMEM((SLOT_ROWS, d), BF16),
            pltpu.VMEM((SLOT_ROWS, d), F32),
        ],
    )
    bgu = b_gate_up.reshape(n_e, 1, f2)
    return pl.pallas_call(
        functools.partial(_expert_body, n_f=n_f),
        grid_spec=grid_spec,
        out_shape=jax.ShapeDtypeStruct(xs.shape, BF16),
        compiler_params=_params(2),
        name="experts",
    )(item_e, item_rows, n_active, xs, w_gate_up, w_gate_up, bgu, bgu, w_down,
      b_down.reshape(n_e, 1, d))


def _combine_body(pos_ref, pos_next_ref, gt_ref, h_ref, g_ref, b_ref, ys_ref, o_ref, buf_ref, sem,
                  *, alpha):
    i = pl.program_id(0)
    slot = i % 2

    def gather(p_ref, s):
        def issue(t, carry):
            for k in range(TOP_K):
                pltpu.make_async_copy(
                    ys_ref.at[p_ref[k, t]], buf_ref.at[s, k, t], sem.at[s]).start(priority=k % 2)
            return carry

        lax.fori_loop(0, TB_COMB, issue, 0, unroll=4)

    @pl.when(i == 0)
    def _():
        gather(pos_ref, 0)

    @pl.when(i + 1 < pl.num_programs(0))
    def _():
        gather(pos_next_ref, 1 - slot)

    for k in range(TOP_K):
        pltpu.make_async_copy(
            ys_ref.at[pl.ds(0, TB_COMB)], buf_ref.at[slot, k], sem.at[slot]).wait()

    ffn = jnp.zeros(h_ref.shape, F32)
    for k in range(TOP_K):
        ffn = ffn + gt_ref[:, k:k + 1] * _from_row_tiles(buf_ref[slot, k]).astype(F32)
    y = alpha * h_ref[...] + ffn
    mu = jnp.mean(y, axis=-1, keepdims=True)
    dev = y - mu
    var = jnp.mean(dev * dev, axis=-1, keepdims=True)
    o_ref[...] = dev * lax.rsqrt(var + LN_EPS) * g_ref[...] + b_ref[...]


def _combine(pos, gates_t, h1, ln_g, ln_b, ys, *, alpha):
    n_tok, d = h1.shape
    const = lambda i: (0, 0)
    n_tiles = n_tok // TB_COMB
    return pl.pallas_call(
        functools.partial(_combine_body, alpha=alpha),
        grid=(n_tiles,),
        in_specs=[
            pl.BlockSpec((TOP_K, TB_COMB), lambda i: (0, i), memory_space=pltpu.SMEM),
            pl.BlockSpec((TOP_K, TB_COMB), lambda i: (0, jnp.minimum(i + 1, n_tiles - 1)),
                         memory_space=pltpu.SMEM),
            pl.BlockSpec((TB_COMB, TOP_K), lambda i: (i, 0)),
            pl.BlockSpec((TB_COMB, d), lambda i: (i, 0)),
            pl.BlockSpec((1, d), const),
            pl.BlockSpec((1, d), const),
            pl.BlockSpec(memory_space=pl.ANY),
        ],
        out_specs=pl.BlockSpec((TB_COMB, d), lambda i: (i, 0)),
        out_shape=jax.ShapeDtypeStruct((n_tok, d), F32),
        scratch_shapes=[pltpu.VMEM((2, TOP_K, TB_COMB) + ys.shape[1:], BF16),
                        pltpu.SemaphoreType.DMA((2,))],
        compiler_params=_params(1),
        name="combine",
    )(pos, pos, gates_t, h1, ln_g.reshape(1, -1), ln_b.reshape(1, -1), ys)


def _rope_tables(seq, half):
    inv_freq = ROPE_BASE ** (-jnp.arange(half, dtype=F32) / half)
    ang = jnp.arange(seq, dtype=F32)[:, None] * inv_freq[None, :]
    return jnp.cos(ang), jnp.sin(ang)


def _layer(h, p, *, batch, seq, alpha, cos, sin):
    n_tok, d = h.shape
    conv_w = p["conv_w"].shape[1]
    ret_w = d - conv_w
    u, q, k, v, g = _in_proj(h, p["w_in"], p["b_in"], cos, sin, seq=seq, conv_w=conv_w, ret_w=ret_w)
    a = _conv_branch(u, p["conv_w"], p["conv_b"], p["conv_ln_g"], p["conv_ln_b"], batch=batch, seq=seq)
    r = _retention(q, k, v, g, batch=batch, seq=seq)
    h1, h1_packed, logits_t = _out_proj(a, r, p["w_out"], p["b_out"], h, p["ln1_g"], p["ln1_b"],
                                        p["w_router"], p["b_router"], alpha=alpha)
    pos, gates, meta = _route(logits_t)
    w_max = N_EXPERTS + (n_tok * TOP_K) // SLOT_ROWS
    xs = _dispatch(pos, h1_packed, w_max * SLOT_ROWS)
    ys = _experts(meta[0, :w_max], meta[1, :w_max], meta[2, :1], xs,
                  p["w_gate_up"], p["b_gate_up"], p["w_down"], p["b_down"], w_max=w_max)
    return _combine(pos, gates.T, h1, p["ln2_g"], p["ln2_b"], ys, alpha=alpha)


def kernel(x, w_in, b_in, conv_w, conv_b, conv_ln_g, conv_ln_b, w_out, b_out, ln1_g, ln1_b,
           w_router, b_router, w_gate_up, b_gate_up, w_down, b_down, ln2_g, ln2_b):
    batch, seq, d = x.shape
    depth = w_in.shape[0]
    alpha = float((2 * depth) ** 0.25)
    head_dim = (d - conv_w.shape[2]) // RET_HEADS
    cos, sin = _rope_tables(seq, head_dim // 2)
    stacked = dict(w_in=w_in, b_in=b_in, conv_w=conv_w, conv_b=conv_b, conv_ln_g=conv_ln_g,
                   conv_ln_b=conv_ln_b, w_out=w_out, b_out=b_out, ln1_g=ln1_g, ln1_b=ln1_b,
                   w_router=w_router, b_router=b_router, w_gate_up=w_gate_up, b_gate_up=b_gate_up,
                   w_down=w_down, b_down=b_down, ln2_g=ln2_g, ln2_b=ln2_b)
    h = x.reshape(batch * seq, d)
    for layer in range(depth):
        p = {name: val[layer] for name, val in stacked.items()}
        h = _layer(h, p, batch=batch, seq=seq, alpha=alpha, cos=cos, sin=sin)
    return h.reshape(batch, seq, d)
```

```python
import functools
import math

import jax
import jax.numpy as jnp
from jax import lax
from jax.experimental import pallas as pl
from jax.experimental.pallas import tpu as pltpu

F32 = jnp.float32
BF16 = jnp.bfloat16
I32 = jnp.int32

RET_HEADS = 4
CONV_KERNEL = 31
ROPE_BASE = 10000.0
N_EXPERTS = 32
TOP_K = 4
SWIGLU_ALPHA = 1.702
SWIGLU_LIMIT = 7.0
LN_EPS = 1e-5
LOG_GAMMA = tuple(math.log(1.0 - 2.0 ** (-5.0 - h)) for h in range(RET_HEADS))

LANES = 128
SUBLANES = 8
VMEM_LIMIT = 58 * 1024 * 1024

TM_IN = 1024
TN_IN = 512
TS_CONV = 256
HALO = 32
RC_CONV = 32
CH_RET = 256
TM_OUT = 512
PREFIX_BLK = 256
SLOT_ROWS = 1280
ROW_BLK = 128
TF_EXP = 256
TB_DISP = 512
TB_COMB = 256


def _params(n_axes):
    return pltpu.CompilerParams(
        dimension_semantics=("arbitrary",) * n_axes, vmem_limit_bytes=VMEM_LIMIT)


def _sigmoid(x):
    return 1.0 / (1.0 + jnp.exp(-x))


def _to_row_tiles(v):
    return v.astype(BF16).reshape(v.shape[0], v.shape[1] // LANES, LANES)


def _from_row_tiles(t):
    return t.reshape(t.shape[0], t.shape[1] * t.shape[2])


def _in_proj_body(x_ref, wa_ref, wb_ref, ba_ref, bb_ref, cos_ref, sin_ref,
                  u_ref, q_ref, k_ref, v_ref, g_ref, xb_ref, *, n_a, n_r, head_dim):
    j = pl.program_id(1)

    @pl.when(j == 0)
    def _():
        xb_ref[...] = x_ref[...].astype(BF16)

    pieces = [slice(c, c + head_dim) for c in range(0, TN_IN, head_dim)]

    def project(w_ref, b_ref, cols):
        return jnp.dot(xb_ref[...], w_ref[:, cols].astype(BF16),
                       preferred_element_type=F32) + b_ref[:, cols]

    @pl.when(j < n_a)
    def _():
        for cols in pieces:
            val = project(wa_ref, ba_ref, cols)
            gate = project(wb_ref, bb_ref, cols)
            u_ref[:, cols] = (val * _sigmoid(gate)).astype(BF16)

    def rotary(o_ref, scale):
        half = head_dim // 2
        for cols in pieces:
            acc = project(wa_ref, ba_ref, cols)
            t1 = acc[:, :half]
            t2 = acc[:, half:]
            cos = cos_ref[...]
            sin = sin_ref[...]
            o_ref[:, cols.start:cols.start + half] = ((t1 * cos - t2 * sin) * scale).astype(BF16)
            o_ref[:, cols.start + half:cols.stop] = ((t1 * sin + t2 * cos) * scale).astype(BF16)

    @pl.when((j >= n_a) & (j < n_a + n_r))
    def _():
        rotary(q_ref, 1.0)

    @pl.when((j >= n_a + n_r) & (j < n_a + 2 * n_r))
    def _():
        rotary(k_ref, head_dim ** -0.5)

    @pl.when((j >= n_a + 2 * n_r) & (j < n_a + 3 * n_r))
    def _():
        for cols in pieces:
            v_ref[:, cols] = project(wa_ref, ba_ref, cols).astype(BF16)

    @pl.when(j >= n_a + 3 * n_r)
    def _():
        for cols in pieces:
            acc = project(wa_ref, ba_ref, cols)
            g_ref[:, cols] = (acc * _sigmoid(acc)).astype(BF16)


def _in_proj(x2, w_in, b_in, cos, sin, *, seq, conv_w, ret_w):
    n, d = x2.shape
    head_dim = ret_w // RET_HEADS
    n_a = conv_w // TN_IN
    n_r = ret_w // TN_IN
    n_j = n_a + 4 * n_r
    seq_tiles = seq // TM_IN
    b2 = b_in.reshape(1, -1)

    def wa_idx(i, j):
        return (0, jnp.where(j < n_a, j, j + n_a))

    def wb_idx(i, j):
        return (0, jnp.where(j < n_a, n_a + j, 2 * n_a - 1))

    def seg(lo, cnt):
        return lambda i, j: (i, jnp.clip(j - lo, 0, cnt - 1))

    out_a = jax.ShapeDtypeStruct((n, conv_w), BF16)
    out_r = jax.ShapeDtypeStruct((n, ret_w), BF16)
    return pl.pallas_call(
        functools.partial(_in_proj_body, n_a=n_a, n_r=n_r, head_dim=head_dim),
        grid=(n // TM_IN, n_j),
        in_specs=[
            pl.BlockSpec((TM_IN, d), lambda i, j: (i, 0)),
            pl.BlockSpec((d, TN_IN), wa_idx),
            pl.BlockSpec((d, TN_IN), wb_idx),
            pl.BlockSpec((1, TN_IN), wa_idx),
            pl.BlockSpec((1, TN_IN), wb_idx),
            pl.BlockSpec((TM_IN, head_dim // 2), lambda i, j: (i % seq_tiles, 0)),
            pl.BlockSpec((TM_IN, head_dim // 2), lambda i, j: (i % seq_tiles, 0)),
        ],
        out_specs=[
            pl.BlockSpec((TM_IN, TN_IN), seg(0, n_a)),
            pl.BlockSpec((TM_IN, TN_IN), seg(n_a, n_r)),
            pl.BlockSpec((TM_IN, TN_IN), seg(n_a + n_r, n_r)),
            pl.BlockSpec((TM_IN, TN_IN), seg(n_a + 2 * n_r, n_r)),
            pl.BlockSpec((TM_IN, TN_IN), seg(n_a + 3 * n_r, n_r)),
        ],
        out_shape=[out_a, out_r, out_r, out_r, out_r],
        scratch_shapes=[pltpu.VMEM((TM_IN, d), BF16)],
        compiler_params=_params(2),
        name="in_proj",
    )(x2, w_in, w_in, b2, b2, cos, sin)


def _conv_body(halo_ref, cur_ref, w_ref, cb_ref, g_ref, b_ref, a_ref, ext_ref, sh_ref):
    si = pl.program_id(1)
    width = cur_ref.shape[1]
    halo = jnp.where(si == 0, 0.0, halo_ref[...].astype(F32))
    ext_ref[0:HALO, :] = halo
    ext_ref[HALO:, :] = cur_ref[...].astype(F32)
    first = HALO - (CONV_KERNEL - 1)
    ext_rows = ext_ref.shape[0]
    for s in range(SUBLANES):
        sh_ref[s, 0:ext_rows - SUBLANES, :] = ext_ref[s:s + ext_rows - SUBLANES, :]
    sh_ref[0, ext_rows - SUBLANES:, :] = ext_ref[ext_rows - SUBLANES:, :]

    def chunk(c, carry):
        r0 = pl.multiple_of(c * RC_CONV, RC_CONV)
        acc = jnp.broadcast_to(cb_ref[...], (RC_CONV, width))
        for j in range(CONV_KERNEL):
            off = first + j
            rows = pl.ds(pl.multiple_of(r0 + (off - off % SUBLANES), SUBLANES), RC_CONV)
            acc = acc + w_ref[j:j + 1, :] * sh_ref[off % SUBLANES, rows, :]
        mu = jnp.mean(acc, axis=-1, keepdims=True)
        dev = acc - mu
        var = jnp.mean(dev * dev, axis=-1, keepdims=True)
        y = dev * lax.rsqrt(var + LN_EPS) * g_ref[...] + b_ref[...]
        a_ref[pl.ds(r0, RC_CONV), :] = (y * _sigmoid(y)).astype(BF16)
        return carry

    lax.fori_loop(0, TS_CONV // RC_CONV, chunk, 0)


def _conv_branch(u, conv_w, conv_b, ln_g, ln_b, *, batch, seq):
    n, width = u.shape
    tiles = seq // TS_CONV

    def halo_idx(b, si):
        return (jnp.maximum(b * (seq // HALO) + si * (TS_CONV // HALO) - 1, 0), 0)

    row = lambda b, si: (0, 0)
    return pl.pallas_call(
        _conv_body,
        grid=(batch, tiles),
        in_specs=[
            pl.BlockSpec((HALO, width), halo_idx),
            pl.BlockSpec((TS_CONV, width), lambda b, si: (b * tiles + si, 0)),
            pl.BlockSpec((CONV_KERNEL, width), row),
            pl.BlockSpec((1, width), row),
            pl.BlockSpec((1, width), row),
            pl.BlockSpec((1, width), row),
        ],
        out_specs=pl.BlockSpec((TS_CONV, width), lambda b, si: (b * tiles + si, 0)),
        out_shape=jax.ShapeDtypeStruct((n, width), BF16),
        scratch_shapes=[pltpu.VMEM((HALO + TS_CONV, width), F32),
                        pltpu.VMEM((SUBLANES, HALO + TS_CONV, width), F32)],
        compiler_params=_params(2),
        name="conv_branch",
    )(u, u, conv_w, conv_b.reshape(1, -1), ln_g.reshape(1, -1), ln_b.reshape(1, -1))


def _retention_body(q_ref, k_ref, v_ref, g_ref, r_ref, state_ref, mask_ref, *, head_dim):
    b = pl.program_id(0)
    c = pl.program_id(1)

    @pl.when((b == 0) & (c == 0))
    def _():
        rel = (lax.broadcasted_iota(I32, (CH_RET, CH_RET), 0)
               - lax.broadcasted_iota(I32, (CH_RET, CH_RET), 1)).astype(F32)
        for h in range(RET_HEADS):
            mask_ref[h] = jnp.where(rel >= 0, jnp.exp(LOG_GAMMA[h] * jnp.maximum(rel, 0.0)), 0.0)

    @pl.when(c == 0)
    def _():
        state_ref[...] = jnp.zeros_like(state_ref)

    idx = lax.broadcasted_iota(I32, (CH_RET, 1), 0).astype(F32)
    nt = (((1,), (1,)), ((), ()))
    tn = (((0,), (0,)), ((), ()))
    for h in range(RET_HEADS):
        sl = slice(h * head_dim, (h + 1) * head_dim)
        qh = q_ref[:, sl]
        kh = k_ref[:, sl]
        vh = v_ref[:, sl]
        scores = lax.dot_general(qh, kh, nt, preferred_element_type=F32) * mask_ref[h]
        inner = jnp.dot(scores.astype(BF16), vh, preferred_element_type=F32)
        st = state_ref[h]
        q_decay = jnp.exp(LOG_GAMMA[h] * (idx + 1.0))
        cross = jnp.dot(qh, st.astype(BF16), preferred_element_type=F32) * q_decay
        k_decay = jnp.exp(LOG_GAMMA[h] * (CH_RET - 1.0 - idx))
        k_dec = (kh.astype(F32) * k_decay).astype(BF16)
        kv = lax.dot_general(k_dec, vh, tn, preferred_element_type=F32)
        state_ref[h] = math.exp(LOG_GAMMA[h] * CH_RET) * st + kv
        o = inner + cross
        mu = jnp.mean(o, axis=-1, keepdims=True)
        dev = o - mu
        var = jnp.mean(dev * dev, axis=-1, keepdims=True)
        r_ref[:, sl] = (g_ref[:, sl].astype(F32) * (dev * lax.rsqrt(var + LN_EPS))).astype(BF16)


def _retention(q, k, v, g, *, batch, seq):
    n, width = q.shape
    head_dim = width // RET_HEADS
    chunks = seq // CH_RET
    spec = pl.BlockSpec((CH_RET, width), lambda b, c: (b * chunks + c, 0))
    return pl.pallas_call(
        functools.partial(_retention_body, head_dim=head_dim),
        grid=(batch, chunks),
        in_specs=[spec, spec, spec, spec],
        out_specs=spec,
        out_shape=jax.ShapeDtypeStruct((n, width), BF16),
        scratch_shapes=[pltpu.VMEM((RET_HEADS, head_dim, head_dim), F32),
                        pltpu.VMEM((RET_HEADS, CH_RET, CH_RET), F32)],
        compiler_params=_params(2),
        name="retention",
    )(q, k, v, g)


def _out_proj_body(a_ref, r_ref, wt_ref, wb_ref, bo_ref, x_ref, g_ref, b_ref, wr_ref, br_ref,
                   h_ref, hp_ref, lt_ref, *, alpha):
    mix = (jnp.dot(a_ref[...], wt_ref[...], preferred_element_type=F32)
           + jnp.dot(r_ref[...], wb_ref[...], preferred_element_type=F32) + bo_ref[...])
    y = alpha * x_ref[...] + mix
    mu = jnp.mean(y, axis=-1, keepdims=True)
    dev = y - mu
    var = jnp.mean(dev * dev, axis=-1, keepdims=True)
    h1 = dev * lax.rsqrt(var + LN_EPS) * g_ref[...] + b_ref[...]
    h_ref[...] = h1
    hp_ref[...] = _to_row_tiles(h1)
    h_hi = h1.astype(BF16)
    h_lo = (h1 - h_hi.astype(F32)).astype(BF16)
    wr = wr_ref[...]
    w_hi = wr.astype(BF16)
    w_lo = (wr - w_hi.astype(F32)).astype(BF16)
    logits = (jnp.dot(h_hi, w_hi, preferred_element_type=F32)
              + jnp.dot(h_hi, w_lo, preferred_element_type=F32)
              + jnp.dot(h_lo, w_hi, preferred_element_type=F32) + br_ref[...])
    lt_ref[...] = logits.T[:N_EXPERTS, :]


def _out_proj(a, r, w_out, b_out, x2, ln_g, ln_b, w_router, b_router, *, alpha):
    n, d = x2.shape
    cw = a.shape[1]
    rw = r.shape[1]
    w_bf = w_out.astype(BF16)
    wr_pad = jnp.pad(w_router, ((0, 0), (0, LANES - N_EXPERTS)))
    br_pad = jnp.pad(b_router, (0, LANES - N_EXPERTS)).reshape(1, LANES)
    const = lambda i: (0, 0)
    return pl.pallas_call(
        functools.partial(_out_proj_body, alpha=alpha),
        grid=(n // TM_OUT,),
        in_specs=[
            pl.BlockSpec((TM_OUT, cw), lambda i: (i, 0)),
            pl.BlockSpec((TM_OUT, rw), lambda i: (i, 0)),
            pl.BlockSpec((cw, d), const),
            pl.BlockSpec((rw, d), lambda i: (cw // rw, 0)),
            pl.BlockSpec((1, d), const),
            pl.BlockSpec((TM_OUT, d), lambda i: (i, 0)),
            pl.BlockSpec((1, d), const),
            pl.BlockSpec((1, d), const),
            pl.BlockSpec((d, LANES), const),
            pl.BlockSpec((1, LANES), const),
        ],
        out_specs=[
            pl.BlockSpec((TM_OUT, d), lambda i: (i, 0)),
            pl.BlockSpec((TM_OUT, d // LANES, LANES), lambda i: (i, 0, 0)),
            pl.BlockSpec((N_EXPERTS, TM_OUT), lambda i: (0, i)),
        ],
        out_shape=[jax.ShapeDtypeStruct((n, d), F32),
                   jax.ShapeDtypeStruct((n, d // LANES, LANES), BF16),
                   jax.ShapeDtypeStruct((N_EXPERTS, n), F32)],
        compiler_params=_params(1),
        name="out_proj",
    )(a, r, w_bf, w_bf, b_out.reshape(1, -1), x2, ln_g.reshape(1, -1), ln_b.reshape(1, -1),
      wr_pad, br_pad)


def _route_body(lt_ref, pos_ref, gate_ref, meta_ref, rank_ref):
    n_tok = lt_ref.shape[1]
    slot = float(SLOT_ROWS)
    l = lt_ref[...]
    eio = lax.broadcasted_iota(I32, (N_EXPERTS, n_tok), 0).astype(F32)
    vals, idxs = [], []
    for _ in range(TOP_K):
        m = jnp.max(l, axis=0, keepdims=True)
        idx = jnp.min(jnp.where(l == m, eio, float(N_EXPERTS)), axis=0, keepdims=True)
        vals.append(m)
        idxs.append(idx)
        l = jnp.where(eio == idx, -jnp.inf, l)
    exps = [jnp.exp(v - vals[0]) for v in vals]
    den = exps[0] + exps[1] + exps[2] + exps[3]
    for k in range(TOP_K):
        gate_ref[k:k + 1, :] = exps[k] / den

    chosen = jnp.zeros((N_EXPERTS, n_tok), F32)
    for k in range(TOP_K):
        chosen = jnp.where(eio == idxs[k], 1.0, chosen)
    tri = (lax.broadcasted_iota(I32, (PREFIX_BLK, PREFIX_BLK), 0)
           < lax.broadcasted_iota(I32, (PREFIX_BLK, PREFIX_BLK), 1)).astype(BF16)
    count = jnp.zeros((N_EXPERTS, 1), F32)
    for blk in range(n_tok // PREFIX_BLK):
        cb = chosen[:, blk * PREFIX_BLK:(blk + 1) * PREFIX_BLK]
        pre = jnp.dot(cb.astype(BF16), tri, preferred_element_type=F32)
        rank_ref[:, blk * PREFIX_BLK:(blk + 1) * PREFIX_BLK] = pre + count
        count = count + jnp.sum(cb, axis=1, keepdims=True)

    n_items = jnp.broadcast_to(jnp.floor((count + (slot - 1.0)) / slot), (N_EXPERTS, LANES))
    rio = lax.broadcasted_iota(I32, (N_EXPERTS, LANES), 0)
    inc = n_items
    shift = 1
    while shift < N_EXPERTS:
        inc = inc + jnp.where(rio >= shift, pltpu.roll(inc, shift, axis=0), 0.0)
        shift *= 2
    start = inc - n_items

    rank = rank_ref[...]
    rq = jnp.floor((rank + 0.5) / slot)
    dest = (start[:, :1] + rq) * slot + (rank - rq * slot)
    for k in range(TOP_K):
        pos_ref[k:k + 1, :] = jnp.sum(
            jnp.where(eio == idxs[k], dest, 0.0), axis=0, keepdims=True).astype(I32)

    wio = lax.broadcasted_iota(I32, (N_EXPERTS, LANES), 1).astype(F32)
    owns = (wio >= start) & (wio < start + n_items)
    rows = jnp.clip(count - (wio - start) * slot, 0.0, slot)
    item_e = jnp.sum(jnp.where(owns, rio.astype(F32), 0.0), axis=0, keepdims=True)
    item_rows = jnp.sum(jnp.where(owns, rows, 0.0), axis=0, keepdims=True)
    meta_ref[...] = jnp.zeros_like(meta_ref)
    meta_ref[0:1, :] = item_e.astype(I32)
    meta_ref[1:2, :] = item_rows.astype(I32)
    meta_ref[2:3, :] = inc[N_EXPERTS - 1:N_EXPERTS, :].astype(I32)


def _route(lt):
    n_tok = lt.shape[1]
    return pl.pallas_call(
        _route_body,
        out_shape=[jax.ShapeDtypeStruct((TOP_K, n_tok), I32),
                   jax.ShapeDtypeStruct((TOP_K, n_tok), F32),
                   jax.ShapeDtypeStruct((8, LANES), I32)],
        scratch_shapes=[pltpu.VMEM((N_EXPERTS, n_tok), F32)],
        compiler_params=pltpu.CompilerParams(vmem_limit_bytes=VMEM_LIMIT),
        name="route",
    )(lt)


def _dispatch_body(pos_ref, h_ref, xs_ref, sem):
    def row_copy(t, k):
        return pltpu.make_async_copy(h_ref.at[t], xs_ref.at[pos_ref[k, t]], sem)

    def issue(t, carry):
        for k in range(TOP_K):
            row_copy(t, k).start(priority=k % 2)
        return carry

    lax.fori_loop(0, TB_DISP, issue, 0, unroll=4)
    for _ in range(TOP_K):
        pltpu.make_async_copy(h_ref, xs_ref.at[pl.ds(0, TB_DISP)], sem).wait()


def _dispatch(pos, hp, n_rows):
    n_tok, subl, lanes = hp.shape
    return pl.pallas_call(
        _dispatch_body,
        grid=(n_tok // TB_DISP,),
        in_specs=[
            pl.BlockSpec((TOP_K, TB_DISP), lambda i: (0, i), memory_space=pltpu.SMEM),
            pl.BlockSpec((TB_DISP, subl, lanes), lambda i: (i, 0, 0)),
        ],
        out_specs=pl.BlockSpec(memory_space=pl.ANY),
        out_shape=jax.ShapeDtypeStruct((n_rows, subl, lanes), BF16),
        scratch_shapes=[pltpu.SemaphoreType.DMA],
        compiler_params=_params(1),
        name="dispatch",
    )(pos, hp)


def _expert_body(ie_ref, nr_ref, na_ref, xs_ref, wg_ref, wl_ref, bg_ref, bl_ref, wd_ref, bd_ref,
                 ys_ref, xb_ref, acc_ref, *, n_f):
    w = pl.program_id(0)
    f = pl.program_id(1)

    @pl.when(w < na_ref[0])
    def _():
        n_rows = nr_ref[w]

        @pl.when(f == 0)
        def _():
            live = lax.broadcasted_iota(I32, xs_ref.shape, 0) < n_rows
            xb_ref[...] = _from_row_tiles(jnp.where(live, xs_ref[...], jnp.zeros((), BF16)))

        bg = bg_ref[0]
        bl = bl_ref[0]

        def mlp_rows(n):
            xblk = xb_ref[0:n, :]
            hg = jnp.dot(xblk, wg_ref[0].astype(BF16), preferred_element_type=F32) + bg
            hl = jnp.dot(xblk, wl_ref[0].astype(BF16), preferred_element_type=F32) + bl
            x_glu = jnp.minimum(hg, SWIGLU_LIMIT)
            x_lin = jnp.clip(hl, -SWIGLU_LIMIT, SWIGLU_LIMIT)
            act = x_glu * _sigmoid(SWIGLU_ALPHA * x_glu) * (x_lin + 1.0)
            start = jnp.where(f == 0, bd_ref[0], acc_ref[0:n, :])
            acc_ref[0:n, :] = start + jnp.dot(
                act.astype(BF16), wd_ref[0].astype(BF16), preferred_element_type=F32)

            @pl.when(f == n_f - 1)
            def _():
                ys_ref[0:n] = _to_row_tiles(acc_ref[0:n, :])

        n_sub = (n_rows + (ROW_BLK - 1)) // ROW_BLK
        for m in range(1, SLOT_ROWS // ROW_BLK + 1):
            @pl.when(n_sub == m)
            def _(m=m):
                mlp_rows(m * ROW_BLK)


def _experts(item_e, item_rows, n_active, xs, w_gate_up, b_gate_up, w_down, b_down, *, w_max):
    n_e, d, f2 = w_gate_up.shape
    d_exp = f2 // 2
    n_f = d_exp // TF_EXP
    slot_block = (SLOT_ROWS,) + xs.shape[1:]

    def item(w, na):
        return jnp.minimum(w, na[0] - 1)

    def fcol(w, f, na):
        return jnp.where(w < na[0], f, n_f - 1)

    def slot_idx(w, f, ie, nr, na):
        return (item(w, na), 0, 0)

    grid_spec = pltpu.PrefetchScalarGridSpec(
        num_scalar_prefetch=3,
        grid=(w_max, n_f),
        in_specs=[
            pl.BlockSpec(slot_block, slot_idx),
            pl.BlockSpec((1, d, TF_EXP), lambda w, f, ie, nr, na: (ie[item(w, na)], 0, fcol(w, f, na))),
            pl.BlockSpec((1, d, TF_EXP),
                         lambda w, f, ie, nr, na: (ie[item(w, na)], 0, n_f + fcol(w, f, na))),
            pl.BlockSpec((1, 1, TF_EXP), lambda w, f, ie, nr, na: (ie[item(w, na)], 0, fcol(w, f, na))),
            pl.BlockSpec((1, 1, TF_EXP),
                         lambda w, f, ie, nr, na: (ie[item(w, na)], 0, n_f + fcol(w, f, na))),
            pl.BlockSpec((1, TF_EXP, d), lambda w, f, ie, nr, na: (ie[item(w, na)], fcol(w, f, na), 0)),
            pl.BlockSpec((1, 1, d), lambda w, f, ie, nr, na: (ie[item(w, na)], 0, 0)),
        ],
        out_specs=pl.BlockSpec(slot_block, slot_idx),
        scratch_shapes=[
            pltpu.VMEM((SLOT_ROWS, d), BF16),
            pltpu.VMEM((SLOT_ROWS, d), F32),
        ],
    )
    bgu = b_gate_up.reshape(n_e, 1, f2)
    return pl.pallas_call(
        functools.partial(_expert_body, n_f=n_f),
        grid_spec=grid_spec,
        out_shape=jax.ShapeDtypeStruct(xs.shape, BF16),
        compiler_params=_params(2),
        name="experts",
    )(item_e, item_rows, n_active, xs, w_gate_up, w_gate_up, bgu, bgu, w_down,
      b_down.reshape(n_e, 1, d))


def _combine_body(pos_ref, pos_next_ref, gt_ref, h_ref, g_ref, b_ref, ys_ref, o_ref, buf_ref, sem,
                  *, alpha):
    i = pl.program_id(0)
    slot = i % 2

    def gather(p_ref, s):
        def issue(t, carry):
            for k in range(TOP_K):
                pltpu.make_async_copy(
                    ys_ref.at[p_ref[k, t]], buf_ref.at[s, k, t], sem.at[s]).start(priority=k % 2)
            return carry

        lax.fori_loop(0, TB_COMB, issue, 0, unroll=4)

    @pl.when(i == 0)
    def _():
        gather(pos_ref, 0)

    @pl.when(i + 1 < pl.num_programs(0))
    def _():
        gather(pos_next_ref, 1 - slot)

    for k in range(TOP_K):
        pltpu.make_async_copy(
            ys_ref.at[pl.ds(0, TB_COMB)], buf_ref.at[slot, k], sem.at[slot]).wait()

    ffn = jnp.zeros(h_ref.shape, F32)
    for k in range(TOP_K):
        ffn = ffn + gt_ref[:, k:k + 1] * _from_row_tiles(buf_ref[slot, k]).astype(F32)
    y = alpha * h_ref[...] + ffn
    mu = jnp.mean(y, axis=-1, keepdims=True)
    dev = y - mu
    var = jnp.mean(dev * dev, axis=-1, keepdims=True)
    o_ref[...] = dev * lax.rsqrt(var + LN_EPS) * g_ref[...] + b_ref[...]


def _combine(pos, gates_t, h1, ln_g, ln_b, ys, *, alpha):
    n_tok, d = h1.shape
    const = lambda i: (0, 0)
    n_tiles = n_tok // TB_COMB
    return pl.pallas_call(
        functools.partial(_combine_body, alpha=alpha),
        grid=(n_tiles,),
        in_specs=[
            pl.BlockSpec((TOP_K, TB_COMB), lambda i: (0, i), memory_space=pltpu.SMEM),
            pl.BlockSpec((TOP_K, TB_COMB), lambda i: (0, jnp.minimum(i + 1, n_tiles - 1)),
                         memory_space=pltpu.SMEM),
            pl.BlockSpec((TB_COMB, TOP_K), lambda i: (i, 0)),
            pl.BlockSpec((TB_COMB, d), lambda i: (i, 0)),
            pl.BlockSpec((1, d), const),
            pl.BlockSpec((1, d), const),
            pl.BlockSpec(memory_space=pl.ANY),
        ],
        out_specs=pl.BlockSpec((TB_COMB, d), lambda i: (i, 0)),
        out_shape=jax.ShapeDtypeStruct((n_tok, d), F32),
        scratch_shapes=[pltpu.VMEM((2, TOP_K, TB_COMB) + ys.shape[1:], BF16),
                        pltpu.SemaphoreType.DMA((2,))],
        compiler_params=_params(1),
        name="combine",
    )(pos, pos, gates_t, h1, ln_g.reshape(1, -1), ln_b.reshape(1, -1), ys)


def _rope_tables(seq, half):
    inv_freq = ROPE_BASE ** (-jnp.arange(half, dtype=F32) / half)
    ang = jnp.arange(seq, dtype=F32)[:, None] * inv_freq[None, :]
    return jnp.cos(ang), jnp.sin(ang)


def _layer(h, p, *, batch, seq, alpha, cos, sin):
    n_tok, d = h.shape
    conv_w = p["conv_w"].shape[1]
    ret_w = d - conv_w
    u, q, k, v, g = _in_proj(h, p["w_in"], p["b_in"], cos, sin, seq=seq, conv_w=conv_w, ret_w=ret_w)
    a = _conv_branch(u, p["conv_w"], p["conv_b"], p["conv_ln_g"], p["conv_ln_b"], batch=batch, seq=seq)
    r = _retention(q, k, v, g, batch=batch, seq=seq)
    h1, h1_packed, logits_t = _out_proj(a, r, p["w_out"], p["b_out"], h, p["ln1_g"], p["ln1_b"],
                                        p["w_router"], p["b_router"], alpha=alpha)
    pos, gates, meta = _route(logits_t)
    w_max = N_EXPERTS + (n_tok * TOP_K) // SLOT_ROWS
    xs = _dispatch(pos, h1_packed, w_max * SLOT_ROWS)
    ys = _experts(meta[0, :w_max], meta[1, :w_max], meta[2, :1], xs,
                  p["w_gate_up"], p["b_gate_up"], p["w_down"], p["b_down"], w_max=w_max)
    return _combine(pos, gates.T, h1, p["ln2_g"], p["ln2_b"], ys, alpha=alpha)


def kernel(x, w_in, b_in, conv_w, conv_b, conv_ln_g, conv_ln_b, w_out, b_out, ln1_g, ln1_b,
           w_router, b_router, w_gate_up, b_gate_up, w_down, b_down, ln2_g, ln2_b):
    batch, seq, d = x.shape
    depth = w_in.shape[0]
    alpha = float((2 * depth) ** 0.25)
    head_dim = (d - conv_w.shape[2]) // RET_HEADS
    cos, sin = _rope_tables(seq, head_dim // 2)
    stacked = dict(w_in=w_in, b_in=b_in, conv_w=conv_w, conv_b=conv_b, conv_ln_g=conv_ln_g,
                   conv_ln_b=conv_ln_b, w_out=w_out, b_out=b_out, ln1_g=ln1_g, ln1_b=ln1_b,
                   w_router=w_router, b_router=b_router, w_gate_up=w_gate_up, b_gate_up=b_gate_up,
                   w_down=w_down, b_down=b_down, ln2_g=ln2_g, ln2_b=ln2_b)
    h = x.reshape(batch * seq, d)
    for layer in range(depth):
        p = {name: val[layer] for name, val in stacked.items()}
        h = _layer(h, p, batch=batch, seq=seq, alpha=alpha, cos=cos, sin=sin)
    return h.reshape(batch, seq, d)
```

```python
import functools
import math

import jax
import jax.numpy as jnp
from jax import lax
from jax.experimental import pallas as pl
from jax.experimental.pallas import tpu as pltpu

F32 = jnp.float32
BF16 = jnp.bfloat16
I32 = jnp.int32

RET_HEADS = 4
CONV_KERNEL = 31
ROPE_BASE = 10000.0
N_EXPERTS = 32
TOP_K = 4
SWIGLU_ALPHA = 1.702
SWIGLU_LIMIT = 7.0
LN_EPS = 1e-5
LOG_GAMMA = tuple(math.log(1.0 - 2.0 ** (-5.0 - h)) for h in range(RET_HEADS))

LANES = 128
SUBLANES = 8
VMEM_LIMIT = 58 * 1024 * 1024

TM_IN = 1024
TN_IN = 512
TS_CONV = 256
HALO = 32
RC_CONV = 32
CH_RET = 256
TM_OUT = 512
PREFIX_BLK = 256
SLOT_ROWS = 1280
ROW_BLK = 128
TF_EXP = 256
TB_DISP = 512
TB_COMB = 256


def _params(n_axes):
    return pltpu.CompilerParams(
        dimension_semantics=("arbitrary",) * n_axes, vmem_limit_bytes=VMEM_LIMIT)


def _sigmoid(x):
    return 1.0 / (1.0 + jnp.exp(-x))


def _to_row_tiles(v):
    return v.astype(BF16).reshape(v.shape[0], v.shape[1] // LANES, LANES)


def _from_row_tiles(t):
    return t.reshape(t.shape[0], t.shape[1] * t.shape[2])


def _in_proj_body(x_ref, wa_ref, wb_ref, ba_ref, bb_ref, cos_ref, sin_ref,
                  u_ref, q_ref, k_ref, v_ref, g_ref, xb_ref, *, n_a, n_r, head_dim):
    j = pl.program_id(1)

    @pl.when(j == 0)
    def _():
        xb_ref[...] = x_ref[...].astype(BF16)

    pieces = [slice(c, c + head_dim) for c in range(0, TN_IN, head_dim)]

    def project(w_ref, b_ref, cols):
        return jnp.dot(xb_ref[...], w_ref[:, cols], preferred_element_type=F32) + b_ref[:, cols]

    @pl.when(j < n_a)
    def _():
        for cols in pieces:
            val = project(wa_ref, ba_ref, cols)
            gate = project(wb_ref, bb_ref, cols)
            u_ref[:, cols] = (val * _sigmoid(gate)).astype(BF16)

    def rotary(o_ref, scale):
        half = head_dim // 2
        for cols in pieces:
            acc = project(wa_ref, ba_ref, cols)
            t1 = acc[:, :half]
            t2 = acc[:, half:]
            cos = cos_ref[...]
            sin = sin_ref[...]
            o_ref[:, cols.start:cols.start + half] = ((t1 * cos - t2 * sin) * scale).astype(BF16)
            o_ref[:, cols.start + half:cols.stop] = ((t1 * sin + t2 * cos) * scale).astype(BF16)

    @pl.when((j >= n_a) & (j < n_a + n_r))
    def _():
        rotary(q_ref, 1.0)

    @pl.when((j >= n_a + n_r) & (j < n_a + 2 * n_r))
    def _():
        rotary(k_ref, head_dim ** -0.5)

    @pl.when((j >= n_a + 2 * n_r) & (j < n_a + 3 * n_r))
    def _():
        for cols in pieces:
            v_ref[:, cols] = project(wa_ref, ba_ref, cols).astype(BF16)

    @pl.when(j >= n_a + 3 * n_r)
    def _():
        for cols in pieces:
            acc = project(wa_ref, ba_ref, cols)
            g_ref[:, cols] = (acc * _sigmoid(acc)).astype(BF16)


def _in_proj(x2, w_in, b_in, cos, sin, *, seq, conv_w, ret_w):
    n, d = x2.shape
    head_dim = ret_w // RET_HEADS
    n_a = conv_w // TN_IN
    n_r = ret_w // TN_IN
    n_j = n_a + 4 * n_r
    seq_tiles = seq // TM_IN
    b2 = b_in.reshape(1, -1)
    w_bf = w_in.astype(BF16)

    def wa_idx(i, j):
        return (0, jnp.where(j < n_a, j, j + n_a))

    def wb_idx(i, j):
        return (0, jnp.where(j < n_a, n_a + j, 2 * n_a - 1))

    def seg(lo, cnt):
        return lambda i, j: (i, jnp.clip(j - lo, 0, cnt - 1))

    out_a = jax.ShapeDtypeStruct((n, conv_w), BF16)
    out_r = jax.ShapeDtypeStruct((n, ret_w), BF16)
    return pl.pallas_call(
        functools.partial(_in_proj_body, n_a=n_a, n_r=n_r, head_dim=head_dim),
        grid=(n // TM_IN, n_j),
        in_specs=[
            pl.BlockSpec((TM_IN, d), lambda i, j: (i, 0)),
            pl.BlockSpec((d, TN_IN), wa_idx),
            pl.BlockSpec((d, TN_IN), wb_idx),
            pl.BlockSpec((1, TN_IN), wa_idx),
            pl.BlockSpec((1, TN_IN), wb_idx),
            pl.BlockSpec((TM_IN, head_dim // 2), lambda i, j: (i % seq_tiles, 0)),
            pl.BlockSpec((TM_IN, head_dim // 2), lambda i, j: (i % seq_tiles, 0)),
        ],
        out_specs=[
            pl.BlockSpec((TM_IN, TN_IN), seg(0, n_a)),
            pl.BlockSpec((TM_IN, TN_IN), seg(n_a, n_r)),
            pl.BlockSpec((TM_IN, TN_IN), seg(n_a + n_r, n_r)),
            pl.BlockSpec((TM_IN, TN_IN), seg(n_a + 2 * n_r, n_r)),
            pl.BlockSpec((TM_IN, TN_IN), seg(n_a + 3 * n_r, n_r)),
        ],
        out_shape=[out_a, out_r, out_r, out_r, out_r],
        scratch_shapes=[pltpu.VMEM((TM_IN, d), BF16)],
        compiler_params=_params(2),
        name="in_proj",
    )(x2, w_bf, w_bf, b2, b2, cos, sin)


def _conv_body(halo_ref, cur_ref, w_ref, cb_ref, g_ref, b_ref, a_ref, ext_ref, sh_ref):
    si = pl.program_id(1)
    width = cur_ref.shape[1]
    halo = jnp.where(si == 0, 0.0, halo_ref[...].astype(F32))
    ext_ref[0:HALO, :] = halo
    ext_ref[HALO:, :] = cur_ref[...].astype(F32)
    first = HALO - (CONV_KERNEL - 1)
    ext_rows = ext_ref.shape[0]
    for s in range(SUBLANES):
        sh_ref[s, 0:ext_rows - SUBLANES, :] = ext_ref[s:s + ext_rows - SUBLANES, :]
    sh_ref[0, ext_rows - SUBLANES:, :] = ext_ref[ext_rows - SUBLANES:, :]

    def chunk(c, carry):
        r0 = pl.multiple_of(c * RC_CONV, RC_CONV)
        acc = jnp.broadcast_to(cb_ref[...], (RC_CONV, width))
        for j in range(CONV_KERNEL):
            off = first + j
            rows = pl.ds(pl.multiple_of(r0 + (off - off % SUBLANES), SUBLANES), RC_CONV)
            acc = acc + w_ref[j:j + 1, :] * sh_ref[off % SUBLANES, rows, :]
        mu = jnp.mean(acc, axis=-1, keepdims=True)
        dev = acc - mu
        var = jnp.mean(dev * dev, axis=-1, keepdims=True)
        y = dev * lax.rsqrt(var + LN_EPS) * g_ref[...] + b_ref[...]
        a_ref[pl.ds(r0, RC_CONV), :] = (y * _sigmoid(y)).astype(BF16)
        return carry

    lax.fori_loop(0, TS_CONV // RC_CONV, chunk, 0)


def _conv_branch(u, conv_w, conv_b, ln_g, ln_b, *, batch, seq):
    n, width = u.shape
    tiles = seq // TS_CONV

    def halo_idx(b, si):
        return (jnp.maximum(b * (seq // HALO) + si * (TS_CONV // HALO) - 1, 0), 0)

    row = lambda b, si: (0, 0)
    return pl.pallas_call(
        _conv_body,
        grid=(batch, tiles),
        in_specs=[
            pl.BlockSpec((HALO, width), halo_idx),
            pl.BlockSpec((TS_CONV, width), lambda b, si: (b * tiles + si, 0)),
            pl.BlockSpec((CONV_KERNEL, width), row),
            pl.BlockSpec((1, width), row),
            pl.BlockSpec((1, width), row),
            pl.BlockSpec((1, width), row),
        ],
        out_specs=pl.BlockSpec((TS_CONV, width), lambda b, si: (b * tiles + si, 0)),
        out_shape=jax.ShapeDtypeStruct((n, width), BF16),
        scratch_shapes=[pltpu.VMEM((HALO + TS_CONV, width), F32),
                        pltpu.VMEM((SUBLANES, HALO + TS_CONV, width), F32)],
        compiler_params=_params(2),
        name="conv_branch",
    )(u, u, conv_w, conv_b.reshape(1, -1), ln_g.reshape(1, -1), ln_b.reshape(1, -1))


def _retention_body(q_ref, k_ref, v_ref, g_ref, r_ref, state_ref, mask_ref, *, head_dim):
    b = pl.program_id(0)
    c = pl.program_id(1)

    @pl.when((b == 0) & (c == 0))
    def _():
        rel = (lax.broadcasted_iota(I32, (CH_RET, CH_RET), 0)
               - lax.broadcasted_iota(I32, (CH_RET, CH_RET), 1)).astype(F32)
        for h in range(RET_HEADS):
            mask_ref[h] = jnp.where(rel >= 0, jnp.exp(LOG_GAMMA[h] * jnp.maximum(rel, 0.0)), 0.0)

    @pl.when(c == 0)
    def _():
        state_ref[...] = jnp.zeros_like(state_ref)

    idx = lax.broadcasted_iota(I32, (CH_RET, 1), 0).astype(F32)
    nt = (((1,), (1,)), ((), ()))
    tn = (((0,), (0,)), ((), ()))
    for h in range(RET_HEADS):
        sl = slice(h * head_dim, (h + 1) * head_dim)
        qh = q_ref[:, sl]
        kh = k_ref[:, sl]
        vh = v_ref[:, sl]
        scores = lax.dot_general(qh, kh, nt, preferred_element_type=F32) * mask_ref[h]
        inner = jnp.dot(scores.astype(BF16), vh, preferred_element_type=F32)
        st = state_ref[h]
        q_decay = jnp.exp(LOG_GAMMA[h] * (idx + 1.0))
        cross = jnp.dot(qh, st.astype(BF16), preferred_element_type=F32) * q_decay
        k_decay = jnp.exp(LOG_GAMMA[h] * (CH_RET - 1.0 - idx))
        k_dec = (kh.astype(F32) * k_decay).astype(BF16)
        kv = lax.dot_general(k_dec, vh, tn, preferred_element_type=F32)
        state_ref[h] = math.exp(LOG_GAMMA[h] * CH_RET) * st + kv
        o = inner + cross
        mu = jnp.mean(o, axis=-1, keepdims=True)
        dev = o - mu
        var = jnp.mean(dev * dev, axis=-1, keepdims=True)
        r_ref[:, sl] = (g_ref[:, sl].astype(F32) * (dev * lax.rsqrt(var + LN_EPS))).astype(BF16)


def _retention(q, k, v, g, *, batch, seq):
    n, width = q.shape
    head_dim = width // RET_HEADS
    chunks = seq // CH_RET
    spec = pl.BlockSpec((CH_RET, width), lambda b, c: (b * chunks + c, 0))
    return pl.pallas_call(
        functools.partial(_retention_body, head_dim=head_dim),
        grid=(batch, chunks),
        in_specs=[spec, spec, spec, spec],
        out_specs=spec,
        out_shape=jax.ShapeDtypeStruct((n, width), BF16),
        scratch_shapes=[pltpu.VMEM((RET_HEADS, head_dim, head_dim), F32),
                        pltpu.VMEM((RET_HEADS, CH_RET, CH_RET), F32)],
        compiler_params=_params(2),
        name="retention",
    )(q, k, v, g)


def _out_proj_body(a_ref, r_ref, wt_ref, wb_ref, bo_ref, x_ref, g_ref, b_ref, wr_ref, br_ref,
                   h_ref, hp_ref, lt_ref, *, alpha):
    mix = (jnp.dot(a_ref[...], wt_ref[...], preferred_element_type=F32)
           + jnp.dot(r_ref[...], wb_ref[...], preferred_element_type=F32) + bo_ref[...])
    y = alpha * x_ref[...] + mix
    mu = jnp.mean(y, axis=-1, keepdims=True)
    dev = y - mu
    var = jnp.mean(dev * dev, axis=-1, keepdims=True)
    h1 = dev * lax.rsqrt(var + LN_EPS) * g_ref[...] + b_ref[...]
    h_ref[...] = h1
    hp_ref[...] = _to_row_tiles(h1)
    h_hi = h1.astype(BF16)
    h_lo = (h1 - h_hi.astype(F32)).astype(BF16)
    wr = wr_ref[...]
    w_hi = wr.astype(BF16)
    w_lo = (wr - w_hi.astype(F32)).astype(BF16)
    hi_both = jnp.dot(h_hi, jnp.concatenate([w_hi, w_lo], axis=1), preferred_element_type=F32)
    logits = (hi_both[:, :LANES] + hi_both[:, LANES:]
              + jnp.dot(h_lo, w_hi, preferred_element_type=F32) + br_ref[...])
    lt_ref[...] = logits.T[:N_EXPERTS, :]


def _out_proj(a, r, w_out, b_out, x2, ln_g, ln_b, w_router, b_router, *, alpha):
    n, d = x2.shape
    cw = a.shape[1]
    rw = r.shape[1]
    w_bf = w_out.astype(BF16)
    wr_pad = jnp.pad(w_router, ((0, 0), (0, LANES - N_EXPERTS)))
    br_pad = jnp.pad(b_router, (0, LANES - N_EXPERTS)).reshape(1, LANES)
    const = lambda i: (0, 0)
    return pl.pallas_call(
        functools.partial(_out_proj_body, alpha=alpha),
        grid=(n // TM_OUT,),
        in_specs=[
            pl.BlockSpec((TM_OUT, cw), lambda i: (i, 0)),
            pl.BlockSpec((TM_OUT, rw), lambda i: (i, 0)),
            pl.BlockSpec((cw, d), const),
            pl.BlockSpec((rw, d), lambda i: (cw // rw, 0)),
            pl.BlockSpec((1, d), const),
            pl.BlockSpec((TM_OUT, d), lambda i: (i, 0)),
            pl.BlockSpec((1, d), const),
            pl.BlockSpec((1, d), const),
            pl.BlockSpec((d, LANES), const),
            pl.BlockSpec((1, LANES), const),
        ],
        out_specs=[
            pl.BlockSpec((TM_OUT, d), lambda i: (i, 0)),
            pl.BlockSpec((TM_OUT, d // LANES, LANES), lambda i: (i, 0, 0)),
            pl.BlockSpec((N_EXPERTS, TM_OUT), lambda i: (0, i)),
        ],
        out_shape=[jax.ShapeDtypeStruct((n, d), F32),
                   jax.ShapeDtypeStruct((n, d // LANES, LANES), BF16),
                   jax.ShapeDtypeStruct((N_EXPERTS, n), F32)],
        compiler_params=_params(1),
        name="out_proj",
    )(a, r, w_bf, w_bf, b_out.reshape(1, -1), x2, ln_g.reshape(1, -1), ln_b.reshape(1, -1),
      wr_pad, br_pad)


def _route_body(lt_ref, pos_ref, gate_ref, meta_ref, rank_ref):
    n_tok = lt_ref.shape[1]
    slot = float(SLOT_ROWS)
    l = lt_ref[...]
    eio = lax.broadcasted_iota(I32, (N_EXPERTS, n_tok), 0).astype(F32)
    vals, idxs = [], []
    for _ in range(TOP_K):
        m = jnp.max(l, axis=0, keepdims=True)
        idx = jnp.min(jnp.where(l == m, eio, float(N_EXPERTS)), axis=0, keepdims=True)
        vals.append(m)
        idxs.append(idx)
        l = jnp.where(eio == idx, -jnp.inf, l)
    exps = [jnp.exp(v - vals[0]) for v in vals]
    den = exps[0] + exps[1] + exps[2] + exps[3]
    for k in range(TOP_K):
        gate_ref[k:k + 1, :] = exps[k] / den

    chosen = jnp.zeros((N_EXPERTS, n_tok), F32)
    for k in range(TOP_K):
        chosen = jnp.where(eio == idxs[k], 1.0, chosen)
    tri = (lax.broadcasted_iota(I32, (PREFIX_BLK, PREFIX_BLK), 0)
           < lax.broadcasted_iota(I32, (PREFIX_BLK, PREFIX_BLK), 1)).astype(BF16)
    count = jnp.zeros((N_EXPERTS, 1), F32)
    for blk in range(n_tok // PREFIX_BLK):
        cb = chosen[:, blk * PREFIX_BLK:(blk + 1) * PREFIX_BLK]
        pre = jnp.dot(cb.astype(BF16), tri, preferred_element_type=F32)
        rank_ref[:, blk * PREFIX_BLK:(blk + 1) * PREFIX_BLK] = pre + count
        count = count + jnp.sum(cb, axis=1, keepdims=True)

    n_items = jnp.broadcast_to(jnp.floor((count + (slot - 1.0)) / slot), (N_EXPERTS, LANES))
    rio = lax.broadcasted_iota(I32, (N_EXPERTS, LANES), 0)
    inc = n_items
    shift = 1
    while shift < N_EXPERTS:
        inc = inc + jnp.where(rio >= shift, pltpu.roll(inc, shift, axis=0), 0.0)
        shift *= 2
    start = inc - n_items

    rank = rank_ref[...]
    rq = jnp.floor((rank + 0.5) / slot)
    dest = (start[:, :1] + rq) * slot + (rank - rq * slot)
    for k in range(TOP_K):
        pos_ref[k:k + 1, :] = jnp.sum(
            jnp.where(eio == idxs[k], dest, 0.0), axis=0, keepdims=True).astype(I32)

    wio = lax.broadcasted_iota(I32, (N_EXPERTS, LANES), 1).astype(F32)
    owns = (wio >= start) & (wio < start + n_items)
    rows = jnp.clip(count - (wio - start) * slot, 0.0, slot)
    item_e = jnp.sum(jnp.where(owns, rio.astype(F32), 0.0), axis=0, keepdims=True)
    item_rows = jnp.sum(jnp.where(owns, rows, 0.0), axis=0, keepdims=True)
    meta_ref[...] = jnp.zeros_like(meta_ref)
    meta_ref[0:1, :] = item_e.astype(I32)
    meta_ref[1:2, :] = item_rows.astype(I32)
    meta_ref[2:3, :] = inc[N_EXPERTS - 1:N_EXPERTS, :].astype(I32)


def _route(lt):
    n_tok = lt.shape[1]
    return pl.pallas_call(
        _route_body,
        out_shape=[jax.ShapeDtypeStruct((TOP_K, n_tok), I32),
                   jax.ShapeDtypeStruct((TOP_K, n_tok), F32),
                   jax.ShapeDtypeStruct((8, LANES), I32)],
        scratch_shapes=[pltpu.VMEM((N_EXPERTS, n_tok), F32)],
        compiler_params=pltpu.CompilerParams(vmem_limit_bytes=VMEM_LIMIT),
        name="route",
    )(lt)


def _dispatch_body(pos_ref, h_ref, xs_ref, sem):
    def row_copy(t, k):
        return pltpu.make_async_copy(h_ref.at[t], xs_ref.at[pos_ref[k, t]], sem)

    def issue(t, carry):
        for k in range(TOP_K):
            row_copy(t, k).start(priority=k % 2)
        return carry

    lax.fori_loop(0, TB_DISP, issue, 0, unroll=4)
    for _ in range(TOP_K):
        pltpu.make_async_copy(h_ref, xs_ref.at[pl.ds(0, TB_DISP)], sem).wait()


def _dispatch(pos, hp, n_rows):
    n_tok, subl, lanes = hp.shape
    return pl.pallas_call(
        _dispatch_body,
        grid=(n_tok // TB_DISP,),
        in_specs=[
            pl.BlockSpec((TOP_K, TB_DISP), lambda i: (0, i), memory_space=pltpu.SMEM),
            pl.BlockSpec((TB_DISP, subl, lanes), lambda i: (i, 0, 0)),
        ],
        out_specs=pl.BlockSpec(memory_space=pl.ANY),
        out_shape=jax.ShapeDtypeStruct((n_rows, subl, lanes), BF16),
        scratch_shapes=[pltpu.SemaphoreType.DMA],
        compiler_params=_params(1),
        name="dispatch",
    )(pos, hp)


def _expert_body(ie_ref, nr_ref, na_ref, xs_ref, wg_ref, wl_ref, bg_ref, bl_ref, wd_ref, bd_ref,
                 ys_ref, xb_ref, acc_ref, *, n_f):
    w = pl.program_id(0)
    f = pl.program_id(1)

    @pl.when(w < na_ref[0])
    def _():
        n_rows = nr_ref[w]

        @pl.when(f == 0)
        def _():
            def unpack(sb, carry):
                rows = pl.ds(pl.multiple_of(sb * ROW_BLK, ROW_BLK), ROW_BLK)
                blk = xs_ref[rows]
                live = lax.broadcasted_iota(I32, blk.shape, 0) + sb * ROW_BLK < n_rows
                xb_ref[rows, :] = _from_row_tiles(jnp.where(live, blk, jnp.zeros((), BF16)))
                return carry

            lax.fori_loop(0, (n_rows + (ROW_BLK - 1)) // ROW_BLK, unpack, 0)

        bg = bg_ref[0]
        bl = bl_ref[0]

        def mlp_rows(n):
            xblk = xb_ref[0:n, :]
            hg = jnp.dot(xblk, wg_ref[0].astype(BF16), preferred_element_type=F32) + bg
            hl = jnp.dot(xblk, wl_ref[0].astype(BF16), preferred_element_type=F32) + bl
            x_glu = jnp.minimum(hg, SWIGLU_LIMIT)
            x_lin = jnp.clip(hl, -SWIGLU_LIMIT, SWIGLU_LIMIT)
            act = x_glu * _sigmoid(SWIGLU_ALPHA * x_glu) * (x_lin + 1.0)
            start = jnp.where(f == 0, bd_ref[0], acc_ref[0:n, :])
            acc_ref[0:n, :] = start + jnp.dot(
                act.astype(BF16), wd_ref[0].astype(BF16), preferred_element_type=F32)

            @pl.when(f == n_f - 1)
            def _():
                ys_ref[0:n] = _to_row_tiles(acc_ref[0:n, :])

        n_sub = (n_rows + (ROW_BLK - 1)) // ROW_BLK
        for m in range(1, SLOT_ROWS // ROW_BLK + 1):
            @pl.when(n_sub == m)
            def _(m=m):
                mlp_rows(m * ROW_BLK)


def _experts(item_e, item_rows, n_active, xs, w_gate_up, b_gate_up, w_down, b_down, *, w_max):
    n_e, d, f2 = w_gate_up.shape
    d_exp = f2 // 2
    n_f = d_exp // TF_EXP
    slot_block = (SLOT_ROWS,) + xs.shape[1:]

    def item(w, na):
        return jnp.minimum(w, na[0] - 1)

    def fcol(w, f, na):
        return jnp.where(w < na[0], f, n_f - 1)

    def slot_idx(w, f, ie, nr, na):
        return (item(w, na), 0, 0)

    grid_spec = pltpu.PrefetchScalarGridSpec(
        num_scalar_prefetch=3,
        grid=(w_max, n_f),
        in_specs=[
            pl.BlockSpec(slot_block, slot_idx),
            pl.BlockSpec((1, d, TF_EXP), lambda w, f, ie, nr, na: (ie[item(w, na)], 0, fcol(w, f, na))),
            pl.BlockSpec((1, d, TF_EXP),
                         lambda w, f, ie, nr, na: (ie[item(w, na)], 0, n_f + fcol(w, f, na))),
            pl.BlockSpec((1, 1, TF_EXP), lambda w, f, ie, nr, na: (ie[item(w, na)], 0, fcol(w, f, na))),
            pl.BlockSpec((1, 1, TF_EXP),
                         lambda w, f, ie, nr, na: (ie[item(w, na)], 0, n_f + fcol(w, f, na))),
            pl.BlockSpec((1, TF_EXP, d), lambda w, f, ie, nr, na: (ie[item(w, na)], fcol(w, f, na), 0)),
            pl.BlockSpec((1, 1, d), lambda w, f, ie, nr, na: (ie[item(w, na)], 0, 0)),
        ],
        out_specs=pl.BlockSpec(slot_block, slot_idx),
        scratch_shapes=[
            pltpu.VMEM((SLOT_ROWS, d), BF16),
            pltpu.VMEM((SLOT_ROWS, d), F32),
        ],
    )
    bgu = b_gate_up.reshape(n_e, 1, f2)
    return pl.pallas_call(
        functools.partial(_expert_body, n_f=n_f),
        grid_spec=grid_spec,
        out_shape=jax.ShapeDtypeStruct(xs.shape, BF16),
        compiler_params=_params(2),
        name="experts",
    )(item_e, item_rows, n_active, xs, w_gate_up, w_gate_up, bgu, bgu, w_down,
      b_down.reshape(n_e, 1, d))


def _combine_body(pos_ref, pos_next_ref, gt_ref, h_ref, g_ref, b_ref, ys_ref, o_ref, buf_ref, sem,
                  *, alpha):
    i = pl.program_id(0)
    slot = i % 2

    def gather(p_ref, s):
        def issue(t, carry):
            for k in range(TOP_K):
                pltpu.make_async_copy(
                    ys_ref.at[p_ref[k, t]], buf_ref.at[s, k, t], sem.at[s]).start(priority=k % 2)
            return carry

        lax.fori_loop(0, TB_COMB, issue, 0, unroll=4)

    @pl.when(i == 0)
    def _():
        gather(pos_ref, 0)

    @pl.when(i + 1 < pl.num_programs(0))
    def _():
        gather(pos_next_ref, 1 - slot)

    for k in range(TOP_K):
        pltpu.make_async_copy(
            ys_ref.at[pl.ds(0, TB_COMB)], buf_ref.at[slot, k], sem.at[slot]).wait()

    ffn = jnp.zeros(h_ref.shape, F32)
    for k in range(TOP_K):
        ffn = ffn + gt_ref[:, k:k + 1] * _from_row_tiles(buf_ref[slot, k]).astype(F32)
    y = alpha * h_ref[...] + ffn
    mu = jnp.mean(y, axis=-1, keepdims=True)
    dev = y - mu
    var = jnp.mean(dev * dev, axis=-1, keepdims=True)
    o_ref[...] = dev * lax.rsqrt(var + LN_EPS) * g_ref[...] + b_ref[...]


def _combine(pos, gates_t, h1, ln_g, ln_b, ys, *, alpha):
    n_tok, d = h1.shape
    const = lambda i: (0, 0)
    n_tiles = n_tok // TB_COMB
    return pl.pallas_call(
        functools.partial(_combine_body, alpha=alpha),
        grid=(n_tiles,),
        in_specs=[
            pl.BlockSpec((TOP_K, TB_COMB), lambda i: (0, i), memory_space=pltpu.SMEM),
            pl.BlockSpec((TOP_K, TB_COMB), lambda i: (0, jnp.minimum(i + 1, n_tiles - 1)),
                         memory_space=pltpu.SMEM),
            pl.BlockSpec((TB_COMB, TOP_K), lambda i: (i, 0)),
            pl.BlockSpec((TB_COMB, d), lambda i: (i, 0)),
            pl.BlockSpec((1, d), const),
            pl.BlockSpec((1, d), const),
            pl.BlockSpec(memory_space=pl.ANY),
        ],
        out_specs=pl.BlockSpec((TB_COMB, d), lambda i: (i, 0)),
        out_shape=jax.ShapeDtypeStruct((n_tok, d), F32),
        scratch_shapes=[pltpu.VMEM((2, TOP_K, TB_COMB) + ys.shape[1:], BF16),
                        pltpu.SemaphoreType.DMA((2,))],
        compiler_params=_params(1),
        name="combine",
    )(pos, pos, gates_t, h1, ln_g.reshape(1, -1), ln_b.reshape(1, -1), ys)


def _rope_tables(seq, half):
    inv_freq = ROPE_BASE ** (-jnp.arange(half, dtype=F32) / half)
    ang = jnp.arange(seq, dtype=F32)[:, None] * inv_freq[None, :]
    return jnp.cos(ang), jnp.sin(ang)


def _layer(h, p, *, batch, seq, alpha, cos, sin):
    n_tok, d = h.shape
    conv_w = p["conv_w"].shape[1]
    ret_w = d - conv_w
    u, q, k, v, g = _in_proj(h, p["w_in"], p["b_in"], cos, sin, seq=seq, conv_w=conv_w, ret_w=ret_w)
    a = _conv_branch(u, p["conv_w"], p["conv_b"], p["conv_ln_g"], p["conv_ln_b"], batch=batch, seq=seq)
    r = _retention(q, k, v, g, batch=batch, seq=seq)
    h1, h1_packed, logits_t = _out_proj(a, r, p["w_out"], p["b_out"], h, p["ln1_g"], p["ln1_b"],
                                        p["w_router"], p["b_router"], alpha=alpha)
    pos, gates, meta = _route(logits_t)
    w_max = N_EXPERTS + (n_tok * TOP_K) // SLOT_ROWS
    xs = _dispatch(pos, h1_packed, w_max * SLOT_ROWS)
    ys = _experts(meta[0, :w_max], meta[1, :w_max], meta[2, :1], xs,
                  p["w_gate_up"], p["b_gate_up"], p["w_down"], p["b_down"], w_max=w_max)
    return _combine(pos, gates.T, h1, p["ln2_g"], p["ln2_b"], ys, alpha=alpha)


def kernel(x, w_in, b_in, conv_w, conv_b, conv_ln_g, conv_ln_b, w_out, b_out, ln1_g, ln1_b,
           w_router, b_router, w_gate_up, b_gate_up, w_down, b_down, ln2_g, ln2_b):
    batch, seq, d = x.shape
    depth = w_in.shape[0]
    alpha = float((2 * depth) ** 0.25)
    head_dim = (d - conv_w.shape[2]) // RET_HEADS
    cos, sin = _rope_tables(seq, head_dim // 2)
    stacked = dict(w_in=w_in, b_in=b_in, conv_w=conv_w, conv_b=conv_b, conv_ln_g=conv_ln_g,
                   conv_ln_b=conv_ln_b, w_out=w_out, b_out=b_out, ln1_g=ln1_g, ln1_b=ln1_b,
                   w_router=w_router, b_router=b_router, w_gate_up=w_gate_up, b_gate_up=b_gate_up,
                   w_down=w_down, b_down=b_down, ln2_g=ln2_g, ln2_b=ln2_b)
    h = x.reshape(batch * seq, d)
    for layer in range(depth):
        p = {name: val[layer] for name, val in stacked.items()}
        h = _layer(h, p, batch=batch, seq=seq, alpha=alpha, cos=cos, sin=sin)
    return h.reshape(batch, seq, d)
```

```python
import functools
import math

import jax
import jax.numpy as jnp
from jax import lax
from jax.experimental import pallas as pl
from jax.experimental.pallas import tpu as pltpu

F32 = jnp.float32
BF16 = jnp.bfloat16
I32 = jnp.int32

RET_HEADS = 4
CONV_KERNEL = 31
ROPE_BASE = 10000.0
N_EXPERTS = 32
TOP_K = 4
SWIGLU_ALPHA = 1.702
SWIGLU_LIMIT = 7.0
LN_EPS = 1e-5
LOG_GAMMA = tuple(math.log(1.0 - 2.0 ** (-5.0 - h)) for h in range(RET_HEADS))

LANES = 128
SUBLANES = 8
VMEM_LIMIT = 58 * 1024 * 1024

TM_IN = 1024
TN_IN = 512
CONV_PIECE = 128
HALO = 32
RC_CONV = 32
CH_RET = 256
TM_OUT = 512
PREFIX_BLK = 256
SLOT_ROWS = 1280
ROW_BLK = 128
TF_EXP = 256
TB_DISP = 512
TB_COMB = 256


def _params(n_axes):
    return pltpu.CompilerParams(
        dimension_semantics=("arbitrary",) * n_axes, vmem_limit_bytes=VMEM_LIMIT)


def _sigmoid(x):
    return 1.0 / (1.0 + jnp.exp(-x))


def _to_row_tiles(v):
    return v.astype(BF16).reshape(v.shape[0], v.shape[1] // LANES, LANES)


def _from_row_tiles(t):
    return t.reshape(t.shape[0], t.shape[1] * t.shape[2])


def _in_proj_body(x_ref, wa_ref, wb_ref, ba_ref, bb_ref, cos_ref, sin_ref, cw_ref, cb_ref, cg_ref, cbt_ref,
                  a_ref, q_ref, k_ref, v_ref, g_ref, xb_ref, u_ref, halo_ref, ext_ref, sh_ref,
                  *, n_a, n_r, head_dim, seq_tiles):
    i = pl.program_id(0)
    j = pl.program_id(1)
    width = u_ref.shape[1]

    @pl.when(j == 0)
    def _():
        xb_ref[...] = x_ref[...].astype(BF16)
        halo_ref[...] = u_ref[TM_IN - HALO:, :]

    pieces = [slice(c, c + head_dim) for c in range(0, TN_IN, head_dim)]

    def project(w_ref, b_ref, cols):
        return jnp.dot(xb_ref[...], w_ref[:, cols], preferred_element_type=F32) + b_ref[:, cols]

    for ja in range(n_a):
        @pl.when(j == ja)
        def _(ja=ja):
            for cols in pieces:
                val = project(wa_ref, ba_ref, cols)
                gate = project(wb_ref, bb_ref, cols)
                u_ref[:, ja * TN_IN + cols.start:ja * TN_IN + cols.stop] = (
                    val * _sigmoid(gate)).astype(BF16)

    def conv_piece():
        p = j - n_a
        r0 = pl.multiple_of(p * CONV_PIECE, CONV_PIECE)
        above = u_ref[pl.ds(pl.multiple_of(jnp.maximum(r0 - HALO, 0), HALO), HALO), :].astype(F32)
        halo = jnp.where(i % seq_tiles == 0, 0.0, halo_ref[...].astype(F32))
        ext_ref[0:HALO, :] = jnp.where(p == 0, halo, above)
        ext_ref[HALO:, :] = u_ref[pl.ds(r0, CONV_PIECE), :].astype(F32)
        first = HALO - (CONV_KERNEL - 1)
        ext_rows = ext_ref.shape[0]
        for s in range(SUBLANES):
            sh_ref[s, 0:ext_rows - SUBLANES, :] = ext_ref[s:s + ext_rows - SUBLANES, :]
        sh_ref[0, ext_rows - SUBLANES:, :] = ext_ref[ext_rows - SUBLANES:, :]
        for c0 in range(0, CONV_PIECE, RC_CONV):
            acc = jnp.broadcast_to(cb_ref[...], (RC_CONV, width))
            for tap in range(CONV_KERNEL):
                off = c0 + first + tap
                lo = off - off % SUBLANES
                acc = acc + cw_ref[tap:tap + 1, :] * sh_ref[off % SUBLANES, lo:lo + RC_CONV, :]
            mu = jnp.mean(acc, axis=-1, keepdims=True)
            dev = acc - mu
            var = jnp.mean(dev * dev, axis=-1, keepdims=True)
            y = dev * lax.rsqrt(var + LN_EPS) * cg_ref[...] + cbt_ref[...]
            a_ref[pl.ds(r0 + c0, RC_CONV), :] = (y * _sigmoid(y)).astype(BF16)

    def rotary(o_ref, scale):
        half = head_dim // 2
        for cols in pieces:
            acc = project(wa_ref, ba_ref, cols)
            t1 = acc[:, :half]
            t2 = acc[:, half:]
            cos = cos_ref[...]
            sin = sin_ref[...]
            o_ref[:, cols.start:cols.start + half] = ((t1 * cos - t2 * sin) * scale).astype(BF16)
            o_ref[:, cols.start + half:cols.stop] = ((t1 * sin + t2 * cos) * scale).astype(BF16)

    @pl.when((j >= n_a) & (j < n_a + n_r))
    def _():
        conv_piece()
        rotary(q_ref, 1.0)

    @pl.when((j >= n_a + n_r) & (j < n_a + 2 * n_r))
    def _():
        conv_piece()
        rotary(k_ref, head_dim ** -0.5)

    @pl.when((j >= n_a + 2 * n_r) & (j < n_a + 3 * n_r))
    def _():
        conv_piece()
        for cols in pieces:
            v_ref[:, cols] = project(wa_ref, ba_ref, cols).astype(BF16)

    @pl.when(j >= n_a + 3 * n_r)
    def _():
        conv_piece()
        for cols in pieces:
            acc = project(wa_ref, ba_ref, cols)
            g_ref[:, cols] = (acc * _sigmoid(acc)).astype(BF16)


def _in_proj(x2, w_in, b_in, cos, sin, conv_taps, conv_b, conv_g, conv_bt, *, seq, conv_w, ret_w):
    n, d = x2.shape
    head_dim = ret_w // RET_HEADS
    n_a = conv_w // TN_IN
    n_r = ret_w // TN_IN
    n_j = n_a + 4 * n_r
    assert CONV_PIECE * 4 * n_r == TM_IN and seq % TM_IN == 0
    seq_tiles = seq // TM_IN
    b2 = b_in.reshape(1, -1)
    w_bf = w_in.astype(BF16)

    def wa_idx(i, j):
        return (0, jnp.where(j < n_a, j, j + n_a))

    def wb_idx(i, j):
        return (0, jnp.where(j < n_a, n_a + j, 2 * n_a - 1))

    def seg(lo, cnt):
        return lambda i, j: (i, jnp.clip(j - lo, 0, cnt - 1))

    const = lambda i, j: (0, 0)
    out_a = jax.ShapeDtypeStruct((n, conv_w), BF16)
    out_r = jax.ShapeDtypeStruct((n, ret_w), BF16)
    return pl.pallas_call(
        functools.partial(_in_proj_body, n_a=n_a, n_r=n_r, head_dim=head_dim, seq_tiles=seq_tiles),
        grid=(n // TM_IN, n_j),
        in_specs=[
            pl.BlockSpec((TM_IN, d), lambda i, j: (i, 0)),
            pl.BlockSpec((d, TN_IN), wa_idx),
            pl.BlockSpec((d, TN_IN), wb_idx),
            pl.BlockSpec((1, TN_IN), wa_idx),
            pl.BlockSpec((1, TN_IN), wb_idx),
            pl.BlockSpec((TM_IN, head_dim // 2), lambda i, j: (i % seq_tiles, 0)),
            pl.BlockSpec((TM_IN, head_dim // 2), lambda i, j: (i % seq_tiles, 0)),
            pl.BlockSpec((CONV_KERNEL, conv_w), const),
            pl.BlockSpec((1, conv_w), const),
            pl.BlockSpec((1, conv_w), const),
            pl.BlockSpec((1, conv_w), const),
        ],
        out_specs=[
            pl.BlockSpec((TM_IN, conv_w), lambda i, j: (i, 0)),
            pl.BlockSpec((TM_IN, TN_IN), seg(n_a, n_r)),
            pl.BlockSpec((TM_IN, TN_IN), seg(n_a + n_r, n_r)),
            pl.BlockSpec((TM_IN, TN_IN), seg(n_a + 2 * n_r, n_r)),
            pl.BlockSpec((TM_IN, TN_IN), seg(n_a + 3 * n_r, n_r)),
        ],
        out_shape=[out_a, out_r, out_r, out_r, out_r],
        scratch_shapes=[pltpu.VMEM((TM_IN, d), BF16),
                        pltpu.VMEM((TM_IN, conv_w), BF16),
                        pltpu.VMEM((HALO, conv_w), BF16),
                        pltpu.VMEM((HALO + CONV_PIECE, conv_w), F32),
                        pltpu.VMEM((SUBLANES, HALO + CONV_PIECE, conv_w), F32)],
        compiler_params=_params(2),
        name="in_proj",
    )(x2, w_bf, w_bf, b2, b2, cos, sin, conv_taps, conv_b.reshape(1, -1), conv_g.reshape(1, -1),
      conv_bt.reshape(1, -1))


def _retention_body(q_ref, k_ref, v_ref, g_ref, r_ref, state_ref, mask_ref, *, head_dim):
    b = pl.program_id(0)
    c = pl.program_id(1)

    @pl.when((b == 0) & (c == 0))
    def _():
        rel = (lax.broadcasted_iota(I32, (CH_RET, CH_RET), 0)
               - lax.broadcasted_iota(I32, (CH_RET, CH_RET), 1)).astype(F32)
        for h in range(RET_HEADS):
            mask_ref[h] = jnp.where(rel >= 0, jnp.exp(LOG_GAMMA[h] * jnp.maximum(rel, 0.0)), 0.0)

    @pl.when(c == 0)
    def _():
        state_ref[...] = jnp.zeros_like(state_ref)

    idx = lax.broadcasted_iota(I32, (CH_RET, 1), 0).astype(F32)
    nt = (((1,), (1,)), ((), ()))
    tn = (((0,), (0,)), ((), ()))
    for h in range(RET_HEADS):
        sl = slice(h * head_dim, (h + 1) * head_dim)
        qh = q_ref[:, sl]
        kh = k_ref[:, sl]
        vh = v_ref[:, sl]
        scores = lax.dot_general(qh, kh, nt, preferred_element_type=F32) * mask_ref[h]
        inner = jnp.dot(scores.astype(BF16), vh, preferred_element_type=F32)
        st = state_ref[h]
        q_decay = jnp.exp(LOG_GAMMA[h] * (idx + 1.0))
        cross = jnp.dot(qh, st.astype(BF16), preferred_element_type=F32) * q_decay
        k_decay = jnp.exp(LOG_GAMMA[h] * (CH_RET - 1.0 - idx))
        k_dec = (kh.astype(F32) * k_decay).astype(BF16)
        kv = lax.dot_general(k_dec, vh, tn, preferred_element_type=F32)
        state_ref[h] = math.exp(LOG_GAMMA[h] * CH_RET) * st + kv
        o = inner + cross
        mu = jnp.mean(o, axis=-1, keepdims=True)
        dev = o - mu
        var = jnp.mean(dev * dev, axis=-1, keepdims=True)
        r_ref[:, sl] = (g_ref[:, sl].astype(F32) * (dev * lax.rsqrt(var + LN_EPS))).astype(BF16)


def _retention(q, k, v, g, *, batch, seq):
    n, width = q.shape
    head_dim = width // RET_HEADS
    chunks = seq // CH_RET
    spec = pl.BlockSpec((CH_RET, width), lambda b, c: (b * chunks + c, 0))
    return pl.pallas_call(
        functools.partial(_retention_body, head_dim=head_dim),
        grid=(batch, chunks),
        in_specs=[spec, spec, spec, spec],
        out_specs=spec,
        out_shape=jax.ShapeDtypeStruct((n, width), BF16),
        scratch_shapes=[pltpu.VMEM((RET_HEADS, head_dim, head_dim), F32),
                        pltpu.VMEM((RET_HEADS, CH_RET, CH_RET), F32)],
        compiler_params=_params(2),
        name="retention",
    )(q, k, v, g)


def _out_proj_body(a_ref, r_ref, wt_ref, wb_ref, bo_ref, x_ref, g_ref, b_ref, wr_ref, br_ref,
                   h_ref, hp_ref, lt_ref, *, alpha):
    mix = (jnp.dot(a_ref[...], wt_ref[...], preferred_element_type=F32)
           + jnp.dot(r_ref[...], wb_ref[...], preferred_element_type=F32) + bo_ref[...])
    y = alpha * x_ref[...] + mix
    mu = jnp.mean(y, axis=-1, keepdims=True)
    dev = y - mu
    var = jnp.mean(dev * dev, axis=-1, keepdims=True)
    h1 = dev * lax.rsqrt(var + LN_EPS) * g_ref[...] + b_ref[...]
    h_ref[...] = h1
    hp_ref[...] = _to_row_tiles(h1)
    h_hi = h1.astype(BF16)
    h_lo = (h1 - h_hi.astype(F32)).astype(BF16)
    wr = wr_ref[...]
    w_hi = wr.astype(BF16)
    w_lo = (wr - w_hi.astype(F32)).astype(BF16)
    hi_both = jnp.dot(h_hi, jnp.concatenate([w_hi, w_lo], axis=1), preferred_element_type=F32)
    logits = (hi_both[:, :LANES] + hi_both[:, LANES:]
              + jnp.dot(h_lo, w_hi, preferred_element_type=F32) + br_ref[...])
    lt_ref[...] = logits.T[:N_EXPERTS, :]


def _out_proj(a, r, w_out, b_out, x2, ln_g, ln_b, w_router, b_router, *, alpha):
    n, d = x2.shape
    cw = a.shape[1]
    rw = r.shape[1]
    w_bf = w_out.astype(BF16)
    wr_pad = jnp.pad(w_router, ((0, 0), (0, LANES - N_EXPERTS)))
    br_pad = jnp.pad(b_router, (0, LANES - N_EXPERTS)).reshape(1, LANES)
    const = lambda i: (0, 0)
    return pl.pallas_call(
        functools.partial(_out_proj_body, alpha=alpha),
        grid=(n // TM_OUT,),
        in_specs=[
            pl.BlockSpec((TM_OUT, cw), lambda i: (i, 0)),
            pl.BlockSpec((TM_OUT, rw), lambda i: (i, 0)),
            pl.BlockSpec((cw, d), const),
            pl.BlockSpec((rw, d), lambda i: (cw // rw, 0)),
            pl.BlockSpec((1, d), const),
            pl.BlockSpec((TM_OUT, d), lambda i: (i, 0)),
            pl.BlockSpec((1, d), const),
            pl.BlockSpec((1, d), const),
            pl.BlockSpec((d, LANES), const),
            pl.BlockSpec((1, LANES), const),
        ],
        out_specs=[
            pl.BlockSpec((TM_OUT, d), lambda i: (i, 0)),
            pl.BlockSpec((TM_OUT, d // LANES, LANES), lambda i: (i, 0, 0)),
            pl.BlockSpec((N_EXPERTS, TM_OUT), lambda i: (0, i)),
        ],
        out_shape=[jax.ShapeDtypeStruct((n, d), F32),
                   jax.ShapeDtypeStruct((n, d // LANES, LANES), BF16),
                   jax.ShapeDtypeStruct((N_EXPERTS, n), F32)],
        compiler_params=_params(1),
        name="out_proj",
    )(a, r, w_bf, w_bf, b_out.reshape(1, -1), x2, ln_g.reshape(1, -1), ln_b.reshape(1, -1),
      wr_pad, br_pad)


def _route_body(lt_ref, pos_ref, gate_ref, meta_ref, rank_ref):
    n_tok = lt_ref.shape[1]
    slot = float(SLOT_ROWS)
    l = lt_ref[...]
    eio = lax.broadcasted_iota(I32, (N_EXPERTS, n_tok), 0).astype(F32)
    vals, idxs = [], []
    for _ in range(TOP_K):
        m = jnp.max(l, axis=0, keepdims=True)
        idx = jnp.min(jnp.where(l == m, eio, float(N_EXPERTS)), axis=0, keepdims=True)
        vals.append(m)
        idxs.append(idx)
        l = jnp.where(eio == idx, -jnp.inf, l)
    exps = [jnp.exp(v - vals[0]) for v in vals]
    den = exps[0] + exps[1] + exps[2] + exps[3]
    for k in range(TOP_K):
        gate_ref[k:k + 1, :] = exps[k] / den

    chosen = jnp.zeros((N_EXPERTS, n_tok), F32)
    for k in range(TOP_K):
        chosen = jnp.where(eio == idxs[k], 1.0, chosen)
    tri = (lax.broadcasted_iota(I32, (PREFIX_BLK, PREFIX_BLK), 0)
           < lax.broadcasted_iota(I32, (PREFIX_BLK, PREFIX_BLK), 1)).astype(BF16)
    count = jnp.zeros((N_EXPERTS, 1), F32)
    for blk in range(n_tok // PREFIX_BLK):
        cb = chosen[:, blk * PREFIX_BLK:(blk + 1) * PREFIX_BLK]
        pre = jnp.dot(cb.astype(BF16), tri, preferred_element_type=F32)
        rank_ref[:, blk * PREFIX_BLK:(blk + 1) * PREFIX_BLK] = pre + count
        count = count + jnp.sum(cb, axis=1, keepdims=True)

    n_items = jnp.broadcast_to(jnp.floor((count + (slot - 1.0)) / slot), (N_EXPERTS, LANES))
    rio = lax.broadcasted_iota(I32, (N_EXPERTS, LANES), 0)
    inc = n_items
    shift = 1
    while shift < N_EXPERTS:
        inc = inc + jnp.where(rio >= shift, pltpu.roll(inc, shift, axis=0), 0.0)
        shift *= 2
    start = inc - n_items

    rank = rank_ref[...]
    rq = jnp.floor((rank + 0.5) / slot)
    dest = (start[:, :1] + rq) * slot + (rank - rq * slot)
    for k in range(TOP_K):
        pos_ref[k:k + 1, :] = jnp.sum(
            jnp.where(eio == idxs[k], dest, 0.0), axis=0, keepdims=True).astype(I32)

    wio = lax.broadcasted_iota(I32, (N_EXPERTS, LANES), 1).astype(F32)
    owns = (wio >= start) & (wio < start + n_items)
    rows = jnp.clip(count - (wio - start) * slot, 0.0, slot)
    item_e = jnp.sum(jnp.where(owns, rio.astype(F32), 0.0), axis=0, keepdims=True)
    item_rows = jnp.sum(jnp.where(owns, rows, 0.0), axis=0, keepdims=True)
    meta_ref[...] = jnp.zeros_like(meta_ref)
    meta_ref[0:1, :] = item_e.astype(I32)
    meta_ref[1:2, :] = item_rows.astype(I32)
    meta_ref[2:3, :] = inc[N_EXPERTS - 1:N_EXPERTS, :].astype(I32)


def _route(lt):
    n_tok = lt.shape[1]
    return pl.pallas_call(
        _route_body,
        out_shape=[jax.ShapeDtypeStruct((TOP_K, n_tok), I32),
                   jax.ShapeDtypeStruct((TOP_K, n_tok), F32),
                   jax.ShapeDtypeStruct((8, LANES), I32)],
        scratch_shapes=[pltpu.VMEM((N_EXPERTS, n_tok), F32)],
        compiler_params=pltpu.CompilerParams(vmem_limit_bytes=VMEM_LIMIT),
        name="route",
    )(lt)


def _dispatch_body(pos_ref, h_ref, xs_ref, sem):
    def row_copy(t, k):
        return pltpu.make_async_copy(h_ref.at[t], xs_ref.at[pos_ref[k, t]], sem)

    def issue(t, carry):
        for k in range(TOP_K):
            row_copy(t, k).start(priority=k % 2)
        return carry

    lax.fori_loop(0, TB_DISP, issue, 0, unroll=4)
    for _ in range(TOP_K):
        pltpu.make_async_copy(h_ref, xs_ref.at[pl.ds(0, TB_DISP)], sem).wait()


def _dispatch(pos, hp, n_rows):
    n_tok, subl, lanes = hp.shape
    return pl.pallas_call(
        _dispatch_body,
        grid=(n_tok // TB_DISP,),
        in_specs=[
            pl.BlockSpec((TOP_K, TB_DISP), lambda i: (0, i), memory_space=pltpu.SMEM),
            pl.BlockSpec((TB_DISP, subl, lanes), lambda i: (i, 0, 0)),
        ],
        out_specs=pl.BlockSpec(memory_space=pl.ANY),
        out_shape=jax.ShapeDtypeStruct((n_rows, subl, lanes), BF16),
        scratch_shapes=[pltpu.SemaphoreType.DMA],
        compiler_params=_params(1),
        name="dispatch",
    )(pos, hp)


def _expert_body(ie_ref, nr_ref, na_ref, xs_ref, wg_ref, wl_ref, bg_ref, bl_ref, wd_ref, bd_ref,
                 ys_ref, xb_ref, acc_ref, *, n_f):
    w = pl.program_id(0)
    f = pl.program_id(1)

    @pl.when(w < na_ref[0])
    def _():
        n_rows = nr_ref[w]

        @pl.when(f == 0)
        def _():
            live = lax.broadcasted_iota(I32, xs_ref.shape, 0) < n_rows
            xb_ref[...] = _from_row_tiles(jnp.where(live, xs_ref[...], jnp.zeros((), BF16)))

        bg = bg_ref[0]
        bl = bl_ref[0]

        def mlp_rows(n):
            xblk = xb_ref[0:n, :]
            hg = jnp.dot(xblk, wg_ref[0].astype(BF16), preferred_element_type=F32) + bg
            hl = jnp.dot(xblk, wl_ref[0].astype(BF16), preferred_element_type=F32) + bl
            x_glu = jnp.minimum(hg, SWIGLU_LIMIT)
            x_lin = jnp.clip(hl, -SWIGLU_LIMIT, SWIGLU_LIMIT)
            act = x_glu * _sigmoid(SWIGLU_ALPHA * x_glu) * (x_lin + 1.0)
            start = jnp.where(f == 0, bd_ref[0], acc_ref[0:n, :])
            acc_ref[0:n, :] = start + jnp.dot(
                act.astype(BF16), wd_ref[0].astype(BF16), preferred_element_type=F32)

            @pl.when(f == n_f - 1)
            def _():
                ys_ref[0:n] = _to_row_tiles(acc_ref[0:n, :])

        n_sub = (n_rows + (ROW_BLK - 1)) // ROW_BLK
        for m in range(1, SLOT_ROWS // ROW_BLK + 1):
            @pl.when(n_sub == m)
            def _(m=m):
                mlp_rows(m * ROW_BLK)


def _experts(item_e, item_rows, n_active, xs, w_gate_up, b_gate_up, w_down, b_down, *, w_max):
    n_e, d, f2 = w_gate_up.shape
    d_exp = f2 // 2
    n_f = d_exp // TF_EXP
    slot_block = (SLOT_ROWS,) + xs.shape[1:]

    def item(w, na):
        return jnp.minimum(w, na[0] - 1)

    def fcol(w, f, na):
        return jnp.where(w < na[0], f, n_f - 1)

    def slot_idx(w, f, ie, nr, na):
        return (item(w, na), 0, 0)

    grid_spec = pltpu.PrefetchScalarGridSpec(
        num_scalar_prefetch=3,
        grid=(w_max, n_f),
        in_specs=[
            pl.BlockSpec(slot_block, slot_idx),
            pl.BlockSpec((1, d, TF_EXP), lambda w, f, ie, nr, na: (ie[item(w, na)], 0, fcol(w, f, na))),
            pl.BlockSpec((1, d, TF_EXP),
                         lambda w, f, ie, nr, na: (ie[item(w, na)], 0, n_f + fcol(w, f, na))),
            pl.BlockSpec((1, 1, TF_EXP), lambda w, f, ie, nr, na: (ie[item(w, na)], 0, fcol(w, f, na))),
            pl.BlockSpec((1, 1, TF_EXP),
                         lambda w, f, ie, nr, na: (ie[item(w, na)], 0, n_f + fcol(w, f, na))),
            pl.BlockSpec((1, TF_EXP, d), lambda w, f, ie, nr, na: (ie[item(w, na)], fcol(w, f, na), 0)),
            pl.BlockSpec((1, 1, d), lambda w, f, ie, nr, na: (ie[item(w, na)], 0, 0)),
        ],
        out_specs=pl.BlockSpec(slot_block, slot_idx),
        scratch_shapes=[
            pltpu.VMEM((SLOT_ROWS, d), BF16),
            pltpu.VMEM((SLOT_ROWS, d), F32),
        ],
    )
    bgu = b_gate_up.reshape(n_e, 1, f2)
    return pl.pallas_call(
        functools.partial(_expert_body, n_f=n_f),
        grid_spec=grid_spec,
        out_shape=jax.ShapeDtypeStruct(xs.shape, BF16),
        compiler_params=_params(2),
        name="experts",
    )(item_e, item_rows, n_active, xs, w_gate_up, w_gate_up, bgu, bgu, w_down,
      b_down.reshape(n_e, 1, d))


def _combine_body(pos_ref, pos_next_ref, gt_ref, h_ref, g_ref, b_ref, ys_ref, o_ref, buf_ref, sem,
                  *, alpha):
    i = pl.program_id(0)
    slot = i % 2

    def gather(p_ref, s):
        def issue(t, carry):
            for k in range(TOP_K):
                pltpu.make_async_copy(
                    ys_ref.at[p_ref[k, t]], buf_ref.at[s, k, t], sem.at[s]).start(priority=k % 2)
            return carry

        lax.fori_loop(0, TB_COMB, issue, 0, unroll=4)

    @pl.when(i == 0)
    def _():
        gather(pos_ref, 0)

    @pl.when(i + 1 < pl.num_programs(0))
    def _():
        gather(pos_next_ref, 1 - slot)

    for k in range(TOP_K):
        pltpu.make_async_copy(
            ys_ref.at[pl.ds(0, TB_COMB)], buf_ref.at[slot, k], sem.at[slot]).wait()

    ffn = jnp.zeros(h_ref.shape, F32)
    for k in range(TOP_K):
        ffn = ffn + gt_ref[:, k:k + 1] * _from_row_tiles(buf_ref[slot, k]).astype(F32)
    y = alpha * h_ref[...] + ffn
    mu = jnp.mean(y, axis=-1, keepdims=True)
    dev = y - mu
    var = jnp.mean(dev * dev, axis=-1, keepdims=True)
    o_ref[...] = dev * lax.rsqrt(var + LN_EPS) * g_ref[...] + b_ref[...]


def _combine(pos, gates_t, h1, ln_g, ln_b, ys, *, alpha):
    n_tok, d = h1.shape
    const = lambda i: (0, 0)
    n_tiles = n_tok // TB_COMB
    return pl.pallas_call(
        functools.partial(_combine_body, alpha=alpha),
        grid=(n_tiles,),
        in_specs=[
            pl.BlockSpec((TOP_K, TB_COMB), lambda i: (0, i), memory_space=pltpu.SMEM),
            pl.BlockSpec((TOP_K, TB_COMB), lambda i: (0, jnp.minimum(i + 1, n_tiles - 1)),
                         memory_space=pltpu.SMEM),
            pl.BlockSpec((TB_COMB, TOP_K), lambda i: (i, 0)),
            pl.BlockSpec((TB_COMB, d), lambda i: (i, 0)),
            pl.BlockSpec((1, d), const),
            pl.BlockSpec((1, d), const),
            pl.BlockSpec(memory_space=pl.ANY),
        ],
        out_specs=pl.BlockSpec((TB_COMB, d), lambda i: (i, 0)),
        out_shape=jax.ShapeDtypeStruct((n_tok, d), F32),
        scratch_shapes=[pltpu.VMEM((2, TOP_K, TB_COMB) + ys.shape[1:], BF16),
                        pltpu.SemaphoreType.DMA((2,))],
        compiler_params=_params(1),
        name="combine",
    )(pos, pos, gates_t, h1, ln_g.reshape(1, -1), ln_b.reshape(1, -1), ys)


def _rope_tables(seq, half):
    inv_freq = ROPE_BASE ** (-jnp.arange(half, dtype=F32) / half)
    ang = jnp.arange(seq, dtype=F32)[:, None] * inv_freq[None, :]
    return jnp.cos(ang), jnp.sin(ang)


def _layer(h, p, *, batch, seq, alpha, cos, sin):
    n_tok, d = h.shape
    conv_w = p["conv_w"].shape[1]
    ret_w = d - conv_w
    a, q, k, v, g = _in_proj(h, p["w_in"], p["b_in"], cos, sin, p["conv_w"], p["conv_b"],
                             p["conv_ln_g"], p["conv_ln_b"], seq=seq, conv_w=conv_w, ret_w=ret_w)
    r = _retention(q, k, v, g, batch=batch, seq=seq)
    h1, h1_packed, logits_t = _out_proj(a, r, p["w_out"], p["b_out"], h, p["ln1_g"], p["ln1_b"],
                                        p["w_router"], p["b_router"], alpha=alpha)
    pos, gates, meta = _route(logits_t)
    w_max = N_EXPERTS + (n_tok * TOP_K) // SLOT_ROWS
    xs = _dispatch(pos, h1_packed, w_max * SLOT_ROWS)
    ys = _experts(meta[0, :w_max], meta[1, :w_max], meta[2, :1], xs,
                  p["w_gate_up"], p["b_gate_up"], p["w_down"], p["b_down"], w_max=w_max)
    return _combine(pos, gates.T, h1, p["ln2_g"], p["ln2_b"], ys, alpha=alpha)


def kernel(x, w_in, b_in, conv_w, conv_b, conv_ln_g, conv_ln_b, w_out, b_out, ln1_g, ln1_b,
           w_router, b_router, w_gate_up, b_gate_up, w_down, b_down, ln2_g, ln2_b):
    batch, seq, d = x.shape
    depth = w_in.shape[0]
    alpha = float((2 * depth) ** 0.25)
    head_dim = (d - conv_w.shape[2]) // RET_HEADS
    cos, sin = _rope_tables(seq, head_dim // 2)
    stacked = dict(w_in=w_in, b_in=b_in, conv_w=conv_w, conv_b=conv_b, conv_ln_g=conv_ln_g,
                   conv_ln_b=conv_ln_b, w_out=w_out, b_out=b_out, ln1_g=ln1_g, ln1_b=ln1_b,
                   w_router=w_router, b_router=b_router, w_gate_up=w_gate_up, b_gate_up=b_gate_up,
                   w_down=w_down, b_down=b_down, ln2_g=ln2_g, ln2_b=ln2_b)
    h = x.reshape(batch * seq, d)
    for layer in range(depth):
        p = {name: val[layer] for name, val in stacked.items()}
        h = _layer(h, p, batch=batch, seq=seq, alpha=alpha, cos=cos, sin=sin)
    return h.reshape(batch, seq, d)
```

```python
import functools
import math

import jax
import jax.numpy as jnp
from jax import lax
from jax.experimental import pallas as pl
from jax.experimental.pallas import tpu as pltpu

F32 = jnp.float32
BF16 = jnp.bfloat16
I32 = jnp.int32

RET_HEADS = 4
CONV_KERNEL = 31
ROPE_BASE = 10000.0
N_EXPERTS = 32
TOP_K = 4
SWIGLU_ALPHA = 1.702
SWIGLU_LIMIT = 7.0
LN_EPS = 1e-5
LOG_GAMMA = tuple(math.log(1.0 - 2.0 ** (-5.0 - h)) for h in range(RET_HEADS))

LANES = 128
SUBLANES = 8
VMEM_LIMIT = 58 * 1024 * 1024

TM_IN = 1024
TN_IN = 512
CONV_PIECE = 128
HALO = 32
RC_CONV = 32
CH_RET = 256
TM_OUT = 512
PREFIX_BLK = 256
SLOT_ROWS = 1280
ROW_BLK = 128
TF_EXP = 256
TB_DISP = 512
TB_COMB = 256


def _params(n_axes):
    return pltpu.CompilerParams(
        dimension_semantics=("arbitrary",) * n_axes, vmem_limit_bytes=VMEM_LIMIT)


def _sigmoid(x):
    return 1.0 / (1.0 + jnp.exp(-x))


def _to_row_tiles(v):
    return v.astype(BF16).reshape(v.shape[0], v.shape[1] // LANES, LANES)


def _from_row_tiles(t):
    return t.reshape(t.shape[0], t.shape[1] * t.shape[2])


def _in_proj_body(x_ref, wa_ref, wb_ref, ba_ref, bb_ref, cos_ref, sin_ref, cw_ref, cb_ref, cg_ref, cbt_ref,
                  a_ref, q_ref, k_ref, v_ref, g_ref, xb_ref, u_ref, halo_ref, ext_ref, sh_ref,
                  *, n_a, n_r, head_dim, seq_tiles):
    i = pl.program_id(0)
    j = pl.program_id(1)
    width = u_ref.shape[1]

    @pl.when(j == 0)
    def _():
        xb_ref[...] = x_ref[...].astype(BF16)
        halo_ref[...] = u_ref[TM_IN - HALO:, :]

    pieces = [slice(c, c + head_dim) for c in range(0, TN_IN, head_dim)]

    def project(w_ref, b_ref, cols):
        return jnp.dot(xb_ref[...], w_ref[:, cols], preferred_element_type=F32) + b_ref[:, cols]

    for ja in range(n_a):
        @pl.when(j == ja)
        def _(ja=ja):
            for cols in pieces:
                val = project(wa_ref, ba_ref, cols)
                gate = project(wb_ref, bb_ref, cols)
                u_ref[:, ja * TN_IN + cols.start:ja * TN_IN + cols.stop] = (
                    val * _sigmoid(gate)).astype(BF16)

    def conv_piece():
        p = j - n_a
        r0 = pl.multiple_of(p * CONV_PIECE, CONV_PIECE)
        above = u_ref[pl.ds(pl.multiple_of(jnp.maximum(r0 - HALO, 0), HALO), HALO), :].astype(F32)
        halo = jnp.where(i % seq_tiles == 0, 0.0, halo_ref[...].astype(F32))
        ext_ref[0:HALO, :] = jnp.where(p == 0, halo, above)
        ext_ref[HALO:, :] = u_ref[pl.ds(r0, CONV_PIECE), :].astype(F32)
        first = HALO - (CONV_KERNEL - 1)
        ext_rows = ext_ref.shape[0]
        for s in range(SUBLANES):
            sh_ref[s, 0:ext_rows - SUBLANES, :] = ext_ref[s:s + ext_rows - SUBLANES, :]
        sh_ref[0, ext_rows - SUBLANES:, :] = ext_ref[ext_rows - SUBLANES:, :]
        for c0 in range(0, CONV_PIECE, RC_CONV):
            acc = jnp.broadcast_to(cb_ref[...], (RC_CONV, width))
            for tap in range(CONV_KERNEL):
                off = c0 + first + tap
                lo = off - off % SUBLANES
                acc = acc + cw_ref[tap:tap + 1, :] * sh_ref[off % SUBLANES, lo:lo + RC_CONV, :]
            mu = jnp.mean(acc, axis=-1, keepdims=True)
            dev = acc - mu
            var = jnp.mean(dev * dev, axis=-1, keepdims=True)
            y = dev * lax.rsqrt(var + LN_EPS) * cg_ref[...] + cbt_ref[...]
            a_ref[pl.ds(r0 + c0, RC_CONV), :] = (y * _sigmoid(y)).astype(BF16)

    def rotary(o_ref, scale):
        half = head_dim // 2
        for cols in pieces:
            acc = project(wa_ref, ba_ref, cols)
            t1 = acc[:, :half]
            t2 = acc[:, half:]
            cos = cos_ref[...]
            sin = sin_ref[...]
            o_ref[:, cols.start:cols.start + half] = ((t1 * cos - t2 * sin) * scale).astype(BF16)
            o_ref[:, cols.start + half:cols.stop] = ((t1 * sin + t2 * cos) * scale).astype(BF16)

    @pl.when((j >= n_a) & (j < n_a + n_r))
    def _():
        conv_piece()
        rotary(q_ref, 1.0)

    @pl.when((j >= n_a + n_r) & (j < n_a + 2 * n_r))
    def _():
        conv_piece()
        rotary(k_ref, head_dim ** -0.5)

    @pl.when((j >= n_a + 2 * n_r) & (j < n_a + 3 * n_r))
    def _():
        conv_piece()
        for cols in pieces:
            v_ref[:, cols] = project(wa_ref, ba_ref, cols).astype(BF16)

    @pl.when(j >= n_a + 3 * n_r)
    def _():
        conv_piece()
        for cols in pieces:
            acc = project(wa_ref, ba_ref, cols)
            g_ref[:, cols] = (acc * _sigmoid(acc)).astype(BF16)


def _in_proj(x2, w_in, b_in, cos, sin, conv_taps, conv_b, conv_g, conv_bt, *, seq, conv_w, ret_w):
    n, d = x2.shape
    head_dim = ret_w // RET_HEADS
    n_a = conv_w // TN_IN
    n_r = ret_w // TN_IN
    n_j = n_a + 4 * n_r
    assert CONV_PIECE * 4 * n_r == TM_IN and seq % TM_IN == 0
    seq_tiles = seq // TM_IN
    b2 = b_in.reshape(1, -1)
    w_bf = w_in.astype(BF16)

    def wa_idx(i, j):
        return (0, jnp.where(j < n_a, j, j + n_a))

    def wb_idx(i, j):
        return (0, jnp.where(j < n_a, n_a + j, 2 * n_a - 1))

    def seg(lo, cnt):
        return lambda i, j: (i, jnp.clip(j - lo, 0, cnt - 1))

    const = lambda i, j: (0, 0)
    out_a = jax.ShapeDtypeStruct((n, conv_w), BF16)
    out_r = jax.ShapeDtypeStruct((n, ret_w), BF16)
    return pl.pallas_call(
        functools.partial(_in_proj_body, n_a=n_a, n_r=n_r, head_dim=head_dim, seq_tiles=seq_tiles),
        grid=(n // TM_IN, n_j),
        in_specs=[
            pl.BlockSpec((TM_IN, d), lambda i, j: (i, 0)),
            pl.BlockSpec((d, TN_IN), wa_idx),
            pl.BlockSpec((d, TN_IN), wb_idx),
            pl.BlockSpec((1, TN_IN), wa_idx),
            pl.BlockSpec((1, TN_IN), wb_idx),
            pl.BlockSpec((TM_IN, head_dim // 2), lambda i, j: (i % seq_tiles, 0)),
            pl.BlockSpec((TM_IN, head_dim // 2), lambda i, j: (i % seq_tiles, 0)),
            pl.BlockSpec((CONV_KERNEL, conv_w), const),
            pl.BlockSpec((1, conv_w), const),
            pl.BlockSpec((1, conv_w), const),
            pl.BlockSpec((1, conv_w), const),
        ],
        out_specs=[
            pl.BlockSpec((TM_IN, conv_w), lambda i, j: (i, 0)),
            pl.BlockSpec((TM_IN, TN_IN), seg(n_a, n_r)),
            pl.BlockSpec((TM_IN, TN_IN), seg(n_a + n_r, n_r)),
            pl.BlockSpec((TM_IN, TN_IN), seg(n_a + 2 * n_r, n_r)),
            pl.BlockSpec((TM_IN, TN_IN), seg(n_a + 3 * n_r, n_r)),
        ],
        out_shape=[out_a, out_r, out_r, out_r, out_r],
        scratch_shapes=[pltpu.VMEM((TM_IN, d), BF16),
                        pltpu.VMEM((TM_IN, conv_w), BF16),
                        pltpu.VMEM((HALO, conv_w), BF16),
                        pltpu.VMEM((HALO + CONV_PIECE, conv_w), F32),
                        pltpu.VMEM((SUBLANES, HALO + CONV_PIECE, conv_w), F32)],
        compiler_params=_params(2),
        name="in_proj",
    )(x2, w_bf, w_bf, b2, b2, cos, sin, conv_taps, conv_b.reshape(1, -1), conv_g.reshape(1, -1),
      conv_bt.reshape(1, -1))


def _retention_body(q_ref, k_ref, v_ref, g_ref, r_ref, state_ref, mask_ref, *, head_dim):
    b = pl.program_id(0)
    c = pl.program_id(1)

    @pl.when((b == 0) & (c == 0))
    def _():
        rel = (lax.broadcasted_iota(I32, (CH_RET, CH_RET), 0)
               - lax.broadcasted_iota(I32, (CH_RET, CH_RET), 1)).astype(F32)
        for h in range(RET_HEADS):
            mask_ref[h] = jnp.where(rel >= 0, jnp.exp(LOG_GAMMA[h] * jnp.maximum(rel, 0.0)), 0.0)

    @pl.when(c == 0)
    def _():
        state_ref[...] = jnp.zeros_like(state_ref)

    idx = lax.broadcasted_iota(I32, (CH_RET, 1), 0).astype(F32)
    nt = (((1,), (1,)), ((), ()))
    tn = (((0,), (0,)), ((), ()))
    for h in range(RET_HEADS):
        sl = slice(h * head_dim, (h + 1) * head_dim)
        qh = q_ref[:, sl]
        kh = k_ref[:, sl]
        vh = v_ref[:, sl]
        scores = lax.dot_general(qh, kh, nt, preferred_element_type=F32) * mask_ref[h]
        inner = jnp.dot(scores.astype(BF16), vh, preferred_element_type=F32)
        st = state_ref[h]
        q_decay = jnp.exp(LOG_GAMMA[h] * (idx + 1.0))
        cross = jnp.dot(qh, st.astype(BF16), preferred_element_type=F32) * q_decay
        k_decay = jnp.exp(LOG_GAMMA[h] * (CH_RET - 1.0 - idx))
        k_dec = (kh.astype(F32) * k_decay).astype(BF16)
        kv = lax.dot_general(k_dec, vh, tn, preferred_element_type=F32)
        state_ref[h] = math.exp(LOG_GAMMA[h] * CH_RET) * st + kv
        o = inner + cross
        mu = jnp.mean(o, axis=-1, keepdims=True)
        dev = o - mu
        var = jnp.mean(dev * dev, axis=-1, keepdims=True)
        r_ref[:, sl] = (g_ref[:, sl].astype(F32) * (dev * lax.rsqrt(var + LN_EPS))).astype(BF16)


def _retention(q, k, v, g, *, batch, seq):
    n, width = q.shape
    head_dim = width // RET_HEADS
    chunks = seq // CH_RET
    spec = pl.BlockSpec((CH_RET, width), lambda b, c: (b * chunks + c, 0))
    return pl.pallas_call(
        functools.partial(_retention_body, head_dim=head_dim),
        grid=(batch, chunks),
        in_specs=[spec, spec, spec, spec],
        out_specs=spec,
        out_shape=jax.ShapeDtypeStruct((n, width), BF16),
        scratch_shapes=[pltpu.VMEM((RET_HEADS, head_dim, head_dim), F32),
                        pltpu.VMEM((RET_HEADS, CH_RET, CH_RET), F32)],
        compiler_params=_params(2),
        name="retention",
    )(q, k, v, g)


def _out_proj_body(a_ref, r_ref, wt_ref, wb_ref, bo_ref, x_ref, g_ref, b_ref, wr_ref, br_ref,
                   h_ref, hp_ref, lt_ref, *, alpha):
    mix = (jnp.dot(a_ref[...], wt_ref[...], preferred_element_type=F32)
           + jnp.dot(r_ref[...], wb_ref[...], preferred_element_type=F32) + bo_ref[...])
    y = alpha * x_ref[...] + mix
    mu = jnp.mean(y, axis=-1, keepdims=True)
    dev = y - mu
    var = jnp.mean(dev * dev, axis=-1, keepdims=True)
    h1 = dev * lax.rsqrt(var + LN_EPS) * g_ref[...] + b_ref[...]
    h_ref[...] = h1
    hp_ref[...] = _to_row_tiles(h1)
    h_hi = h1.astype(BF16)
    h_lo = (h1 - h_hi.astype(F32)).astype(BF16)
    wr = wr_ref[...]
    w_hi = wr.astype(BF16)
    w_lo = (wr - w_hi.astype(F32)).astype(BF16)
    hi_both = jnp.dot(h_hi, jnp.concatenate([w_hi, w_lo], axis=1), preferred_element_type=F32)
    logits = (hi_both[:, :LANES] + hi_both[:, LANES:]
              + jnp.dot(h_lo, w_hi, preferred_element_type=F32) + br_ref[...])
    lt_ref[...] = logits.T[:N_EXPERTS, :]


def _out_proj(a, r, w_out, b_out, x2, ln_g, ln_b, w_router, b_router, *, alpha):
    n, d = x2.shape
    cw = a.shape[1]
    rw = r.shape[1]
    w_bf = w_out.astype(BF16)
    wr_pad = jnp.pad(w_router, ((0, 0), (0, LANES - N_EXPERTS)))
    br_pad = jnp.pad(b_router, (0, LANES - N_EXPERTS)).reshape(1, LANES)
    const = lambda i: (0, 0)
    return pl.pallas_call(
        functools.partial(_out_proj_body, alpha=alpha),
        grid=(n // TM_OUT,),
        in_specs=[
            pl.BlockSpec((TM_OUT, cw), lambda i: (i, 0)),
            pl.BlockSpec((TM_OUT, rw), lambda i: (i, 0)),
            pl.BlockSpec((cw, d), const),
            pl.BlockSpec((rw, d), lambda i: (cw // rw, 0)),
            pl.BlockSpec((1, d), const),
            pl.BlockSpec((TM_OUT, d), lambda i: (i, 0)),
            pl.BlockSpec((1, d), const),
            pl.BlockSpec((1, d), const),
            pl.BlockSpec((d, LANES), const),
            pl.BlockSpec((1, LANES), const),
        ],
        out_specs=[
            pl.BlockSpec((TM_OUT, d), lambda i: (i, 0)),
            pl.BlockSpec((TM_OUT, d // LANES, LANES), lambda i: (i, 0, 0)),
            pl.BlockSpec((N_EXPERTS, TM_OUT), lambda i: (0, i)),
        ],
        out_shape=[jax.ShapeDtypeStruct((n, d), F32),
                   jax.ShapeDtypeStruct((n, d // LANES, LANES), BF16),
                   jax.ShapeDtypeStruct((N_EXPERTS, n), F32)],
        compiler_params=_params(1),
        name="out_proj",
    )(a, r, w_bf, w_bf, b_out.reshape(1, -1), x2, ln_g.reshape(1, -1), ln_b.reshape(1, -1),
      wr_pad, br_pad)


def _route_body(lt_ref, pos_ref, gate_ref, meta_ref, rank_ref):
    n_tok = lt_ref.shape[1]
    slot = float(SLOT_ROWS)
    l = lt_ref[...]
    eio = lax.broadcasted_iota(I32, (N_EXPERTS, n_tok), 0).astype(F32)
    vals, idxs = [], []
    for _ in range(TOP_K):
        m = jnp.max(l, axis=0, keepdims=True)
        idx = jnp.min(jnp.where(l == m, eio, float(N_EXPERTS)), axis=0, keepdims=True)
        vals.append(m)
        idxs.append(idx)
        l = jnp.where(eio == idx, -jnp.inf, l)
    exps = [jnp.exp(v - vals[0]) for v in vals]
    den = exps[0] + exps[1] + exps[2] + exps[3]
    for k in range(TOP_K):
        gate_ref[k:k + 1, :] = exps[k] / den

    chosen = jnp.zeros((N_EXPERTS, n_tok), F32)
    for k in range(TOP_K):
        chosen = jnp.where(eio == idxs[k], 1.0, chosen)
    tri = (lax.broadcasted_iota(I32, (PREFIX_BLK, PREFIX_BLK), 0)
           < lax.broadcasted_iota(I32, (PREFIX_BLK, PREFIX_BLK), 1)).astype(BF16)
    count = jnp.zeros((N_EXPERTS, 1), F32)
    for blk in range(n_tok // PREFIX_BLK):
        cb = chosen[:, blk * PREFIX_BLK:(blk + 1) * PREFIX_BLK]
        pre = jnp.dot(cb.astype(BF16), tri, preferred_element_type=F32)
        rank_ref[:, blk * PREFIX_BLK:(blk + 1) * PREFIX_BLK] = pre + count
        count = count + jnp.sum(cb, axis=1, keepdims=True)

    n_items = jnp.broadcast_to(jnp.floor((count + (slot - 1.0)) / slot), (N_EXPERTS, LANES))
    rio = lax.broadcasted_iota(I32, (N_EXPERTS, LANES), 0)
    inc = n_items
    shift = 1
    while shift < N_EXPERTS:
        inc = inc + jnp.where(rio >= shift, pltpu.roll(inc, shift, axis=0), 0.0)
        shift *= 2
    start = inc - n_items

    rank = rank_ref[...]
    rq = jnp.floor((rank + 0.5) / slot)
    dest = (start[:, :1] + rq) * slot + (rank - rq * slot)
    for k in range(TOP_K):
        pos_ref[k:k + 1, :] = jnp.sum(
            jnp.where(eio == idxs[k], dest, 0.0), axis=0, keepdims=True).astype(I32)

    wio = lax.broadcasted_iota(I32, (N_EXPERTS, LANES), 1).astype(F32)
    owns = (wio >= start) & (wio < start + n_items)
    rows = jnp.clip(count - (wio - start) * slot, 0.0, slot)
    item_e = jnp.sum(jnp.where(owns, rio.astype(F32), 0.0), axis=0, keepdims=True)
    item_rows = jnp.sum(jnp.where(owns, rows, 0.0), axis=0, keepdims=True)
    meta_ref[...] = jnp.zeros_like(meta_ref)
    meta_ref[0:1, :] = item_e.astype(I32)
    meta_ref[1:2, :] = item_rows.astype(I32)
    meta_ref[2:3, :] = inc[N_EXPERTS - 1:N_EXPERTS, :].astype(I32)


def _route(lt):
    n_tok = lt.shape[1]
    return pl.pallas_call(
        _route_body,
        out_shape=[jax.ShapeDtypeStruct((TOP_K, n_tok), I32),
                   jax.ShapeDtypeStruct((TOP_K, n_tok), F32),
                   jax.ShapeDtypeStruct((8, LANES), I32)],
        scratch_shapes=[pltpu.VMEM((N_EXPERTS, n_tok), F32)],
        compiler_params=pltpu.CompilerParams(vmem_limit_bytes=VMEM_LIMIT),
        name="route",
    )(lt)


def _dispatch_body(pos_ref, h_ref, xs_ref, sem):
    def row_copy(t, k):
        return pltpu.make_async_copy(h_ref.at[t], xs_ref.at[pos_ref[k, t]], sem)

    def issue(t, carry):
        for k in range(TOP_K):
            row_copy(t, k).start(priority=k % 2)
        return carry

    lax.fori_loop(0, TB_DISP, issue, 0, unroll=4)
    for _ in range(TOP_K):
        pltpu.make_async_copy(h_ref, xs_ref.at[pl.ds(0, TB_DISP)], sem).wait()


def _dispatch(pos, hp, n_rows):
    n_tok, subl, lanes = hp.shape
    return pl.pallas_call(
        _dispatch_body,
        grid=(n_tok // TB_DISP,),
        in_specs=[
            pl.BlockSpec((TOP_K, TB_DISP), lambda i: (0, i), memory_space=pltpu.SMEM),
            pl.BlockSpec((TB_DISP, subl, lanes), lambda i: (i, 0, 0)),
        ],
        out_specs=pl.BlockSpec(memory_space=pl.ANY),
        out_shape=jax.ShapeDtypeStruct((n_rows, subl, lanes), BF16),
        scratch_shapes=[pltpu.SemaphoreType.DMA],
        compiler_params=_params(1),
        name="dispatch",
    )(pos, hp)


def _expert_body(ie_ref, nr_ref, na_ref, xs_ref, wg_ref, wl_ref, bg_ref, bl_ref, wd_ref, bd_ref,
                 ys_ref, xb_ref, acc_ref, *, n_f):
    w = pl.program_id(0)
    f = pl.program_id(1)

    @pl.when(w < na_ref[0])
    def _():
        n_rows = nr_ref[w]

        @pl.when(f == 0)
        def _():
            xb_ref[...] = _from_row_tiles(xs_ref[...])
            last0 = ((n_rows + (ROW_BLK - 1)) // ROW_BLK - 1) * ROW_BLK
            last = pl.ds(pl.multiple_of(last0, ROW_BLK), ROW_BLK)
            row = lax.broadcasted_iota(I32, (ROW_BLK, xb_ref.shape[1]), 0) + last0
            xb_ref[last, :] = jnp.where(row < n_rows, xb_ref[last, :], jnp.zeros((), BF16))

        bg = bg_ref[0]
        bl = bl_ref[0]

        def mlp_rows(n):
            xblk = xb_ref[0:n, :]
            hg = jnp.dot(xblk, wg_ref[0].astype(BF16), preferred_element_type=F32) + bg
            hl = jnp.dot(xblk, wl_ref[0].astype(BF16), preferred_element_type=F32) + bl
            x_glu = jnp.minimum(hg, SWIGLU_LIMIT)
            x_lin = jnp.clip(hl, -SWIGLU_LIMIT, SWIGLU_LIMIT)
            act = x_glu * _sigmoid(SWIGLU_ALPHA * x_glu) * (x_lin + 1.0)
            start = jnp.where(f == 0, bd_ref[0], acc_ref[0:n, :])
            acc_ref[0:n, :] = start + jnp.dot(
                act.astype(BF16), wd_ref[0].astype(BF16), preferred_element_type=F32)

            @pl.when(f == n_f - 1)
            def _():
                ys_ref[0:n] = _to_row_tiles(acc_ref[0:n, :])

        n_sub = (n_rows + (ROW_BLK - 1)) // ROW_BLK
        for m in range(1, SLOT_ROWS // ROW_BLK + 1):
            @pl.when(n_sub == m)
            def _(m=m):
                mlp_rows(m * ROW_BLK)


def _experts(item_e, item_rows, n_active, xs, w_gate_up, b_gate_up, w_down, b_down, *, w_max):
    n_e, d, f2 = w_gate_up.shape
    d_exp = f2 // 2
    n_f = d_exp // TF_EXP
    slot_block = (SLOT_ROWS,) + xs.shape[1:]

    def item(w, na):
        return jnp.minimum(w, na[0] - 1)

    def fcol(w, f, na):
        return jnp.where(w < na[0], f, n_f - 1)

    def slot_idx(w, f, ie, nr, na):
        return (item(w, na), 0, 0)

    def slot_in_idx(w, f, ie, nr, na):
        return (item(w + (f >= n_f // 2).astype(I32), na), 0, 0)

    grid_spec = pltpu.PrefetchScalarGridSpec(
        num_scalar_prefetch=3,
        grid=(w_max, n_f),
        in_specs=[
            pl.BlockSpec(slot_block, slot_in_idx),
            pl.BlockSpec((1, d, TF_EXP), lambda w, f, ie, nr, na: (ie[item(w, na)], 0, fcol(w, f, na))),
            pl.BlockSpec((1, d, TF_EXP),
                         lambda w, f, ie, nr, na: (ie[item(w, na)], 0, n_f + fcol(w, f, na))),
            pl.BlockSpec((1, 1, TF_EXP), lambda w, f, ie, nr, na: (ie[item(w, na)], 0, fcol(w, f, na))),
            pl.BlockSpec((1, 1, TF_EXP),
                         lambda w, f, ie, nr, na: (ie[item(w, na)], 0, n_f + fcol(w, f, na))),
            pl.BlockSpec((1, TF_EXP, d), lambda w, f, ie, nr, na: (ie[item(w, na)], fcol(w, f, na), 0)),
            pl.BlockSpec((1, 1, d), lambda w, f, ie, nr, na: (ie[item(w, na)], 0, 0)),
        ],
        out_specs=pl.BlockSpec(slot_block, slot_idx),
        scratch_shapes=[
            pltpu.VMEM((SLOT_ROWS, d), BF16),
            pltpu.VMEM((SLOT_ROWS, d), F32),
        ],
    )
    bgu = b_gate_up.reshape(n_e, 1, f2)
    return pl.pallas_call(
        functools.partial(_expert_body, n_f=n_f),
        grid_spec=grid_spec,
        out_shape=jax.ShapeDtypeStruct(xs.shape, BF16),
        compiler_params=_params(2),
        name="experts",
    )(item_e, item_rows, n_active, xs, w_gate_up, w_gate_up, bgu, bgu, w_down,
      b_down.reshape(n_e, 1, d))


def _combine_body(pos_ref, pos_next_ref, gt_ref, h_ref, g_ref, b_ref, ys_ref, o_ref, buf_ref, sem,
                  *, alpha):
    i = pl.program_id(0)
    slot = i % 2

    def gather(p_ref, s):
        def issue(t, carry):
            for k in range(TOP_K):
                pltpu.make_async_copy(
                    ys_ref.at[p_ref[k, t]], buf_ref.at[s, k, t], sem.at[s]).start(priority=k % 2)
            return carry

        lax.fori_loop(0, TB_COMB, issue, 0, unroll=4)

    @pl.when(i == 0)
    def _():
        gather(pos_ref, 0)

    @pl.when(i + 1 < pl.num_programs(0))
    def _():
        gather(pos_next_ref, 1 - slot)

    for k in range(TOP_K):
        pltpu.make_async_copy(
            ys_ref.at[pl.ds(0, TB_COMB)], buf_ref.at[slot, k], sem.at[slot]).wait()

    ffn = jnp.zeros(h_ref.shape, F32)
    for k in range(TOP_K):
        ffn = ffn + gt_ref[:, k:k + 1] * _from_row_tiles(buf_ref[slot, k]).astype(F32)
    y = alpha * h_ref[...] + ffn
    mu = jnp.mean(y, axis=-1, keepdims=True)
    dev = y - mu
    var = jnp.mean(dev * dev, axis=-1, keepdims=True)
    o_ref[...] = dev * lax.rsqrt(var + LN_EPS) * g_ref[...] + b_ref[...]


def _combine(pos, gates_t, h1, ln_g, ln_b, ys, *, alpha):
    n_tok, d = h1.shape
    const = lambda i: (0, 0)
    n_tiles = n_tok // TB_COMB
    return pl.pallas_call(
        functools.partial(_combine_body, alpha=alpha),
        grid=(n_tiles,),
        in_specs=[
            pl.BlockSpec((TOP_K, TB_COMB), lambda i: (0, i), memory_space=pltpu.SMEM),
            pl.BlockSpec((TOP_K, TB_COMB), lambda i: (0, jnp.minimum(i + 1, n_tiles - 1)),
                         memory_space=pltpu.SMEM),
            pl.BlockSpec((TB_COMB, TOP_K), lambda i: (i, 0)),
            pl.BlockSpec((TB_COMB, d), lambda i: (i, 0)),
            pl.BlockSpec((1, d), const),
            pl.BlockSpec((1, d), const),
            pl.BlockSpec(memory_space=pl.ANY),
        ],
        out_specs=pl.BlockSpec((TB_COMB, d), lambda i: (i, 0)),
        out_shape=jax.ShapeDtypeStruct((n_tok, d), F32),
        scratch_shapes=[pltpu.VMEM((2, TOP_K, TB_COMB) + ys.shape[1:], BF16),
                        pltpu.SemaphoreType.DMA((2,))],
        compiler_params=_params(1),
        name="combine",
    )(pos, pos, gates_t, h1, ln_g.reshape(1, -1), ln_b.reshape(1, -1), ys)


def _rope_tables(seq, half):
    inv_freq = ROPE_BASE ** (-jnp.arange(half, dtype=F32) / half)
    ang = jnp.arange(seq, dtype=F32)[:, None] * inv_freq[None, :]
    return jnp.cos(ang), jnp.sin(ang)


def _layer(h, p, *, batch, seq, alpha, cos, sin):
    n_tok, d = h.shape
    conv_w = p["conv_w"].shape[1]
    ret_w = d - conv_w
    a, q, k, v, g = _in_proj(h, p["w_in"], p["b_in"], cos, sin, p["conv_w"], p["conv_b"],
                             p["conv_ln_g"], p["conv_ln_b"], seq=seq, conv_w=conv_w, ret_w=ret_w)
    r = _retention(q, k, v, g, batch=batch, seq=seq)
    h1, h1_packed, logits_t = _out_proj(a, r, p["w_out"], p["b_out"], h, p["ln1_g"], p["ln1_b"],
                                        p["w_router"], p["b_router"], alpha=alpha)
    pos, gates, meta = _route(logits_t)
    w_max = N_EXPERTS + (n_tok * TOP_K) // SLOT_ROWS
    xs = _dispatch(pos, h1_packed, w_max * SLOT_ROWS)
    ys = _experts(meta[0, :w_max], meta[1, :w_max], meta[2, :1], xs,
                  p["w_gate_up"], p["b_gate_up"], p["w_down"], p["b_down"], w_max=w_max)
    return _combine(pos, gates.T, h1, p["ln2_g"], p["ln2_b"], ys, alpha=alpha)


def kernel(x, w_in, b_in, conv_w, conv_b, conv_ln_g, conv_ln_b, w_out, b_out, ln1_g, ln1_b,
           w_router, b_router, w_gate_up, b_gate_up, w_down, b_down, ln2_g, ln2_b):
    batch, seq, d = x.shape
    depth = w_in.shape[0]
    alpha = float((2 * depth) ** 0.25)
    head_dim = (d - conv_w.shape[2]) // RET_HEADS
    cos, sin = _rope_tables(seq, head_dim // 2)
    stacked = dict(w_in=w_in, b_in=b_in, conv_w=conv_w, conv_b=conv_b, conv_ln_g=conv_ln_g,
                   conv_ln_b=conv_ln_b, w_out=w_out, b_out=b_out, ln1_g=ln1_g, ln1_b=ln1_b,
                   w_router=w_router, b_router=b_router, w_gate_up=w_gate_up, b_gate_up=b_gate_up,
                   w_down=w_down, b_down=b_down, ln2_g=ln2_g, ln2_b=ln2_b)
    h = x.reshape(batch * seq, d)
    for layer in range(depth):
        p = {name: val[layer] for name, val in stacked.items()}
        h = _layer(h, p, batch=batch, seq=seq, alpha=alpha, cos=cos, sin=sin)
    return h.reshape(batch, seq, d)
```

```python
import functools
import math

import jax
import jax.numpy as jnp
from jax import lax
from jax.experimental import pallas as pl
from jax.experimental.pallas import tpu as pltpu

F32 = jnp.float32
BF16 = jnp.bfloat16
I32 = jnp.int32

RET_HEADS = 4
CONV_KERNEL = 31
ROPE_BASE = 10000.0
N_EXPERTS = 32
TOP_K = 4
SWIGLU_ALPHA = 1.702
SWIGLU_LIMIT = 7.0
LN_EPS = 1e-5
LOG_GAMMA = tuple(math.log(1.0 - 2.0 ** (-5.0 - h)) for h in range(RET_HEADS))

LANES = 128
SUBLANES = 8
VMEM_LIMIT = 58 * 1024 * 1024

TM_IN = 1024
TN_IN = 512
CONV_PIECE = 128
HALO = 32
RC_CONV = 32
CH_RET = 256
TM_OUT = 512
PREFIX_BLK = 256
SLOT_ROWS = 1280
ROW_BLK = 128
TF_EXP = 256
TB_DISP = 512
TB_COMB = 256


def _params(n_axes):
    return pltpu.CompilerParams(
        dimension_semantics=("arbitrary",) * n_axes, vmem_limit_bytes=VMEM_LIMIT)


def _sigmoid(x):
    return 1.0 / (1.0 + jnp.exp(-x))


def _to_row_tiles(v):
    return v.astype(BF16).reshape(v.shape[0], v.shape[1] // LANES, LANES)


def _from_row_tiles(t):
    return t.reshape(t.shape[0], t.shape[1] * t.shape[2])


def _in_proj_body(x_ref, wa_ref, wb_ref, ba_ref, bb_ref, cos_ref, sin_ref, cw_ref, cb_ref, cg_ref, cbt_ref,
                  a_ref, q_ref, k_ref, v_ref, g_ref, xb_ref, u_ref, halo_ref, ext_ref, sh_ref,
                  *, n_a, n_r, head_dim, seq_tiles):
    i = pl.program_id(0)
    j = pl.program_id(1)
    width = u_ref.shape[1]

    @pl.when(j == 0)
    def _():
        xb_ref[...] = x_ref[...].astype(BF16)
        halo_ref[...] = u_ref[TM_IN - HALO:, :]

    pieces = [slice(c, c + head_dim) for c in range(0, TN_IN, head_dim)]

    def project(w_ref, b_ref, cols):
        return jnp.dot(xb_ref[...], w_ref[:, cols], preferred_element_type=F32) + b_ref[:, cols]

    for ja in range(n_a):
        @pl.when(j == ja)
        def _(ja=ja):
            for cols in pieces:
                val = project(wa_ref, ba_ref, cols)
                gate = project(wb_ref, bb_ref, cols)
                u_ref[:, ja * TN_IN + cols.start:ja * TN_IN + cols.stop] = (
                    val * _sigmoid(gate)).astype(BF16)

    def conv_piece():
        p = j - n_a
        r0 = pl.multiple_of(p * CONV_PIECE, CONV_PIECE)
        above = u_ref[pl.ds(pl.multiple_of(jnp.maximum(r0 - HALO, 0), HALO), HALO), :].astype(F32)
        halo = jnp.where(i % seq_tiles == 0, 0.0, halo_ref[...].astype(F32))
        ext_ref[0:HALO, :] = jnp.where(p == 0, halo, above)
        ext_ref[HALO:, :] = u_ref[pl.ds(r0, CONV_PIECE), :].astype(F32)
        first = HALO - (CONV_KERNEL - 1)
        ext_rows = ext_ref.shape[0]
        for s in range(SUBLANES):
            sh_ref[s, 0:ext_rows - SUBLANES, :] = ext_ref[s:s + ext_rows - SUBLANES, :]
        sh_ref[0, ext_rows - SUBLANES:, :] = ext_ref[ext_rows - SUBLANES:, :]
        for c0 in range(0, CONV_PIECE, RC_CONV):
            acc = jnp.broadcast_to(cb_ref[...], (RC_CONV, width))
            for tap in range(CONV_KERNEL):
                off = c0 + first + tap
                lo = off - off % SUBLANES
                acc = acc + cw_ref[tap:tap + 1, :] * sh_ref[off % SUBLANES, lo:lo + RC_CONV, :]
            mu = jnp.mean(acc, axis=-1, keepdims=True)
            dev = acc - mu
            var = jnp.mean(dev * dev, axis=-1, keepdims=True)
            y = dev * lax.rsqrt(var + LN_EPS) * cg_ref[...] + cbt_ref[...]
            a_ref[pl.ds(r0 + c0, RC_CONV), :] = (y * _sigmoid(y)).astype(BF16)

    def rotary(o_ref, scale):
        half = head_dim // 2
        for cols in pieces:
            acc = project(wa_ref, ba_ref, cols)
            t1 = acc[:, :half]
            t2 = acc[:, half:]
            cos = cos_ref[...]
            sin = sin_ref[...]
            o_ref[:, cols.start:cols.start + half] = ((t1 * cos - t2 * sin) * scale).astype(BF16)
            o_ref[:, cols.start + half:cols.stop] = ((t1 * sin + t2 * cos) * scale).astype(BF16)

    @pl.when((j >= n_a) & (j < n_a + n_r))
    def _():
        conv_piece()
        rotary(q_ref, 1.0)

    @pl.when((j >= n_a + n_r) & (j < n_a + 2 * n_r))
    def _():
        conv_piece()
        rotary(k_ref, head_dim ** -0.5)

    @pl.when((j >= n_a + 2 * n_r) & (j < n_a + 3 * n_r))
    def _():
        conv_piece()
        for cols in pieces:
            v_ref[:, cols] = project(wa_ref, ba_ref, cols).astype(BF16)

    @pl.when(j >= n_a + 3 * n_r)
    def _():
        conv_piece()
        for cols in pieces:
            acc = project(wa_ref, ba_ref, cols)
            g_ref[:, cols] = (acc * _sigmoid(acc)).astype(BF16)


def _in_proj(x2, w_in, b_in, cos, sin, conv_taps, conv_b, conv_g, conv_bt, *, seq, conv_w, ret_w):
    n, d = x2.shape
    head_dim = ret_w // RET_HEADS
    n_a = conv_w // TN_IN
    n_r = ret_w // TN_IN
    n_j = n_a + 4 * n_r
    assert CONV_PIECE * 4 * n_r == TM_IN and seq % TM_IN == 0
    seq_tiles = seq // TM_IN
    b2 = b_in.reshape(1, -1)
    w_bf = w_in.astype(BF16)

    def wa_idx(i, j):
        return (0, jnp.where(j < n_a, j, j + n_a))

    def wb_idx(i, j):
        return (0, jnp.where(j < n_a, n_a + j, 2 * n_a - 1))

    def seg(lo, cnt):
        return lambda i, j: (i, jnp.clip(j - lo, 0, cnt - 1))

    const = lambda i, j: (0, 0)
    out_a = jax.ShapeDtypeStruct((n, conv_w), BF16)
    out_r = jax.ShapeDtypeStruct((n, ret_w), BF16)
    return pl.pallas_call(
        functools.partial(_in_proj_body, n_a=n_a, n_r=n_r, head_dim=head_dim, seq_tiles=seq_tiles),
        grid=(n // TM_IN, n_j),
        in_specs=[
            pl.BlockSpec((TM_IN, d), lambda i, j: (i, 0)),
            pl.BlockSpec((d, TN_IN), wa_idx),
            pl.BlockSpec((d, TN_IN), wb_idx),
            pl.BlockSpec((1, TN_IN), wa_idx),
            pl.BlockSpec((1, TN_IN), wb_idx),
            pl.BlockSpec((TM_IN, head_dim // 2), lambda i, j: (i % seq_tiles, 0)),
            pl.BlockSpec((TM_IN, head_dim // 2), lambda i, j: (i % seq_tiles, 0)),
            pl.BlockSpec((CONV_KERNEL, conv_w), const),
            pl.BlockSpec((1, conv_w), const),
            pl.BlockSpec((1, conv_w), const),
            pl.BlockSpec((1, conv_w), const),
        ],
        out_specs=[
            pl.BlockSpec((TM_IN, conv_w), lambda i, j: (i, 0)),
            pl.BlockSpec((TM_IN, TN_IN), seg(n_a, n_r)),
            pl.BlockSpec((TM_IN, TN_IN), seg(n_a + n_r, n_r)),
            pl.BlockSpec((TM_IN, TN_IN), seg(n_a + 2 * n_r, n_r)),
            pl.BlockSpec((TM_IN, TN_IN), seg(n_a + 3 * n_r, n_r)),
        ],
        out_shape=[out_a, out_r, out_r, out_r, out_r],
        scratch_shapes=[pltpu.VMEM((TM_IN, d), BF16),
                        pltpu.VMEM((TM_IN, conv_w), BF16),
                        pltpu.VMEM((HALO, conv_w), BF16),
                        pltpu.VMEM((HALO + CONV_PIECE, conv_w), F32),
                        pltpu.VMEM((SUBLANES, HALO + CONV_PIECE, conv_w), F32)],
        compiler_params=_params(2),
        name="in_proj",
    )(x2, w_bf, w_bf, b2, b2, cos, sin, conv_taps, conv_b.reshape(1, -1), conv_g.reshape(1, -1),
      conv_bt.reshape(1, -1))


def _retention_mask_init(mask_ref):
    rel = (lax.broadcasted_iota(I32, (CH_RET, CH_RET), 0)
           - lax.broadcasted_iota(I32, (CH_RET, CH_RET), 1)).astype(F32)
    for h in range(RET_HEADS):
        mask_ref[h] = jnp.where(rel >= 0, jnp.exp(LOG_GAMMA[h] * jnp.maximum(rel, 0.0)), 0.0)


def _retention_chunk(q_ref, k_ref, v_ref, g_ref, r_ref, state_ref, mask_ref, rows, head_dim):
    idx = lax.broadcasted_iota(I32, (CH_RET, 1), 0).astype(F32)
    nt = (((1,), (1,)), ((), ()))
    tn = (((0,), (0,)), ((), ()))
    for h in range(RET_HEADS):
        sl = slice(h * head_dim, (h + 1) * head_dim)
        qh = q_ref[rows, sl]
        kh = k_ref[rows, sl]
        vh = v_ref[rows, sl]
        scores = lax.dot_general(qh, kh, nt, preferred_element_type=F32) * mask_ref[h]
        inner = jnp.dot(scores.astype(BF16), vh, preferred_element_type=F32)
        st = state_ref[h]
        q_decay = jnp.exp(LOG_GAMMA[h] * (idx + 1.0))
        cross = jnp.dot(qh, st.astype(BF16), preferred_element_type=F32) * q_decay
        k_decay = jnp.exp(LOG_GAMMA[h] * (CH_RET - 1.0 - idx))
        k_dec = (kh.astype(F32) * k_decay).astype(BF16)
        kv = lax.dot_general(k_dec, vh, tn, preferred_element_type=F32)
        state_ref[h] = math.exp(LOG_GAMMA[h] * CH_RET) * st + kv
        o = inner + cross
        mu = jnp.mean(o, axis=-1, keepdims=True)
        dev = o - mu
        var = jnp.mean(dev * dev, axis=-1, keepdims=True)
        r_ref[rows, sl] = (g_ref[rows, sl].astype(F32) * (dev * lax.rsqrt(var + LN_EPS))).astype(BF16)


def _out_proj_body(a_ref, q_ref, k_ref, v_ref, gate_ref, wt_ref, wb_ref, bo_ref, x_ref, g_ref, b_ref,
                   wr_ref, br_ref, h_ref, hp_ref, lt_ref, r_ref, state_ref, mask_ref,
                   *, alpha, head_dim, seq_tiles):
    i = pl.program_id(0)

    @pl.when(i == 0)
    def _():
        _retention_mask_init(mask_ref)

    @pl.when(i % seq_tiles == 0)
    def _():
        state_ref[...] = jnp.zeros_like(state_ref)

    for c0 in range(0, TM_OUT, CH_RET):
        _retention_chunk(q_ref, k_ref, v_ref, gate_ref, r_ref, state_ref, mask_ref,
                         slice(c0, c0 + CH_RET), head_dim)
    mix = (jnp.dot(a_ref[...], wt_ref[...], preferred_element_type=F32)
           + jnp.dot(r_ref[...], wb_ref[...], preferred_element_type=F32) + bo_ref[...])
    y = alpha * x_ref[...] + mix
    mu = jnp.mean(y, axis=-1, keepdims=True)
    dev = y - mu
    var = jnp.mean(dev * dev, axis=-1, keepdims=True)
    h1 = dev * lax.rsqrt(var + LN_EPS) * g_ref[...] + b_ref[...]
    h_ref[...] = h1
    hp_ref[...] = _to_row_tiles(h1)
    h_hi = h1.astype(BF16)
    h_lo = (h1 - h_hi.astype(F32)).astype(BF16)
    wr = wr_ref[...]
    w_hi = wr.astype(BF16)
    w_lo = (wr - w_hi.astype(F32)).astype(BF16)
    hi_both = jnp.dot(h_hi, jnp.concatenate([w_hi, w_lo], axis=1), preferred_element_type=F32)
    logits = (hi_both[:, :LANES] + hi_both[:, LANES:]
              + jnp.dot(h_lo, w_hi, preferred_element_type=F32) + br_ref[...])
    lt_ref[...] = logits.T[:N_EXPERTS, :]


def _out_proj(a, q, k, v, gate, w_out, b_out, x2, ln_g, ln_b, w_router, b_router, *, alpha, seq):
    n, d = x2.shape
    cw = a.shape[1]
    rw = q.shape[1]
    head_dim = rw // RET_HEADS
    assert seq % TM_OUT == 0 and TM_OUT % CH_RET == 0
    w_bf = w_out.astype(BF16)
    wr_pad = jnp.pad(w_router, ((0, 0), (0, LANES - N_EXPERTS)))
    br_pad = jnp.pad(b_router, (0, LANES - N_EXPERTS)).reshape(1, LANES)
    const = lambda i: (0, 0)
    ret_spec = pl.BlockSpec((TM_OUT, rw), lambda i: (i, 0))
    return pl.pallas_call(
        functools.partial(_out_proj_body, alpha=alpha, head_dim=head_dim, seq_tiles=seq // TM_OUT),
        grid=(n // TM_OUT,),
        in_specs=[
            pl.BlockSpec((TM_OUT, cw), lambda i: (i, 0)),
            ret_spec, ret_spec, ret_spec, ret_spec,
            pl.BlockSpec((cw, d), const),
            pl.BlockSpec((rw, d), lambda i: (cw // rw, 0)),
            pl.BlockSpec((1, d), const),
            pl.BlockSpec((TM_OUT, d), lambda i: (i, 0)),
            pl.BlockSpec((1, d), const),
            pl.BlockSpec((1, d), const),
            pl.BlockSpec((d, LANES), const),
            pl.BlockSpec((1, LANES), const),
        ],
        out_specs=[
            pl.BlockSpec((TM_OUT, d), lambda i: (i, 0)),
            pl.BlockSpec((TM_OUT, d // LANES, LANES), lambda i: (i, 0, 0)),
            pl.BlockSpec((N_EXPERTS, TM_OUT), lambda i: (0, i)),
        ],
        out_shape=[jax.ShapeDtypeStruct((n, d), F32),
                   jax.ShapeDtypeStruct((n, d // LANES, LANES), BF16),
                   jax.ShapeDtypeStruct((N_EXPERTS, n), F32)],
        scratch_shapes=[pltpu.VMEM((TM_OUT, rw), BF16),
                        pltpu.VMEM((RET_HEADS, head_dim, head_dim), F32),
                        pltpu.VMEM((RET_HEADS, CH_RET, CH_RET), F32)],
        compiler_params=_params(1),
        name="out_proj",
    )(a, q, k, v, gate, w_bf, w_bf, b_out.reshape(1, -1), x2, ln_g.reshape(1, -1),
      ln_b.reshape(1, -1), wr_pad, br_pad)


def _route_body(lt_ref, pos_ref, gate_ref, meta_ref, rank_ref):
    n_tok = lt_ref.shape[1]
    slot = float(SLOT_ROWS)
    l = lt_ref[...]
    eio = lax.broadcasted_iota(I32, (N_EXPERTS, n_tok), 0).astype(F32)
    vals, idxs = [], []
    for _ in range(TOP_K):
        m = jnp.max(l, axis=0, keepdims=True)
        idx = jnp.min(jnp.where(l == m, eio, float(N_EXPERTS)), axis=0, keepdims=True)
        vals.append(m)
        idxs.append(idx)
        l = jnp.where(eio == idx, -jnp.inf, l)
    exps = [jnp.exp(v - vals[0]) for v in vals]
    den = exps[0] + exps[1] + exps[2] + exps[3]
    for k in range(TOP_K):
        gate_ref[k:k + 1, :] = exps[k] / den

    chosen = jnp.zeros((N_EXPERTS, n_tok), F32)
    for k in range(TOP_K):
        chosen = jnp.where(eio == idxs[k], 1.0, chosen)
    tri = (lax.broadcasted_iota(I32, (PREFIX_BLK, PREFIX_BLK), 0)
           < lax.broadcasted_iota(I32, (PREFIX_BLK, PREFIX_BLK), 1)).astype(BF16)
    count = jnp.zeros((N_EXPERTS, 1), F32)
    for blk in range(n_tok // PREFIX_BLK):
        cb = chosen[:, blk * PREFIX_BLK:(blk + 1) * PREFIX_BLK]
        pre = jnp.dot(cb.astype(BF16), tri, preferred_element_type=F32)
        rank_ref[:, blk * PREFIX_BLK:(blk + 1) * PREFIX_BLK] = pre + count
        count = count + jnp.sum(cb, axis=1, keepdims=True)

    n_items = jnp.broadcast_to(jnp.floor((count + (slot - 1.0)) / slot), (N_EXPERTS, LANES))
    rio = lax.broadcasted_iota(I32, (N_EXPERTS, LANES), 0)
    inc = n_items
    shift = 1
    while shift < N_EXPERTS:
        inc = inc + jnp.where(rio >= shift, pltpu.roll(inc, shift, axis=0), 0.0)
        shift *= 2
    start = inc - n_items

    rank = rank_ref[...]
    rq = jnp.floor((rank + 0.5) / slot)
    dest = (start[:, :1] + rq) * slot + (rank - rq * slot)
    for k in range(TOP_K):
        pos_ref[k:k + 1, :] = jnp.sum(
            jnp.where(eio == idxs[k], dest, 0.0), axis=0, keepdims=True).astype(I32)

    wio = lax.broadcasted_iota(I32, (N_EXPERTS, LANES), 1).astype(F32)
    owns = (wio >= start) & (wio < start + n_items)
    rows = jnp.clip(count - (wio - start) * slot, 0.0, slot)
    item_e = jnp.sum(jnp.where(owns, rio.astype(F32), 0.0), axis=0, keepdims=True)
    item_rows = jnp.sum(jnp.where(owns, rows, 0.0), axis=0, keepdims=True)
    meta_ref[...] = jnp.zeros_like(meta_ref)
    meta_ref[0:1, :] = item_e.astype(I32)
    meta_ref[1:2, :] = item_rows.astype(I32)
    meta_ref[2:3, :] = inc[N_EXPERTS - 1:N_EXPERTS, :].astype(I32)


def _route(lt):
    n_tok = lt.shape[1]
    return pl.pallas_call(
        _route_body,
        out_shape=[jax.ShapeDtypeStruct((TOP_K, n_tok), I32),
                   jax.ShapeDtypeStruct((TOP_K, n_tok), F32),
                   jax.ShapeDtypeStruct((8, LANES), I32)],
        scratch_shapes=[pltpu.VMEM((N_EXPERTS, n_tok), F32)],
        compiler_params=pltpu.CompilerParams(vmem_limit_bytes=VMEM_LIMIT),
        name="route",
    )(lt)


def _dispatch_body(pos_ref, h_ref, xs_ref, sem):
    def row_copy(t, k):
        return pltpu.make_async_copy(h_ref.at[t], xs_ref.at[pos_ref[k, t]], sem)

    def issue(t, carry):
        for k in range(TOP_K):
            row_copy(t, k).start(priority=k % 2)
        return carry

    lax.fori_loop(0, TB_DISP, issue, 0, unroll=4)
    for _ in range(TOP_K):
        pltpu.make_async_copy(h_ref, xs_ref.at[pl.ds(0, TB_DISP)], sem).wait()


def _dispatch(pos, hp, n_rows):
    n_tok, subl, lanes = hp.shape
    return pl.pallas_call(
        _dispatch_body,
        grid=(n_tok // TB_DISP,),
        in_specs=[
            pl.BlockSpec((TOP_K, TB_DISP), lambda i: (0, i), memory_space=pltpu.SMEM),
            pl.BlockSpec((TB_DISP, subl, lanes), lambda i: (i, 0, 0)),
        ],
        out_specs=pl.BlockSpec(memory_space=pl.ANY),
        out_shape=jax.ShapeDtypeStruct((n_rows, subl, lanes), BF16),
        scratch_shapes=[pltpu.SemaphoreType.DMA],
        compiler_params=_params(1),
        name="dispatch",
    )(pos, hp)


def _expert_body(ie_ref, nr_ref, na_ref, xs_ref, wg_ref, wl_ref, bg_ref, bl_ref, wd_ref, bd_ref,
                 ys_ref, xb_ref, acc_ref, *, n_f):
    w = pl.program_id(0)
    f = pl.program_id(1)

    @pl.when(w < na_ref[0])
    def _():
        n_rows = nr_ref[w]

        @pl.when(f == 0)
        def _():
            xb_ref[...] = _from_row_tiles(xs_ref[...])
            last0 = ((n_rows + (ROW_BLK - 1)) // ROW_BLK - 1) * ROW_BLK
            last = pl.ds(pl.multiple_of(last0, ROW_BLK), ROW_BLK)
            row = lax.broadcasted_iota(I32, (ROW_BLK, xb_ref.shape[1]), 0) + last0
            xb_ref[last, :] = jnp.where(row < n_rows, xb_ref[last, :], jnp.zeros((), BF16))

        bg = bg_ref[0]
        bl = bl_ref[0]

        def mlp_rows(n):
            xblk = xb_ref[0:n, :]
            hg = jnp.dot(xblk, wg_ref[0].astype(BF16), preferred_element_type=F32) + bg
            hl = jnp.dot(xblk, wl_ref[0].astype(BF16), preferred_element_type=F32) + bl
            x_glu = jnp.minimum(hg, SWIGLU_LIMIT)
            x_lin = jnp.clip(hl, -SWIGLU_LIMIT, SWIGLU_LIMIT)
            act = x_glu * _sigmoid(SWIGLU_ALPHA * x_glu) * (x_lin + 1.0)
            start = jnp.where(f == 0, bd_ref[0], acc_ref[0:n, :])
            acc_ref[0:n, :] = start + jnp.dot(
                act.astype(BF16), wd_ref[0].astype(BF16), preferred_element_type=F32)

            @pl.when(f == n_f - 1)
            def _():
                ys_ref[0:n] = _to_row_tiles(acc_ref[0:n, :])

        n_sub = (n_rows + (ROW_BLK - 1)) // ROW_BLK
        for m in range(1, SLOT_ROWS // ROW_BLK + 1):
            @pl.when(n_sub == m)
            def _(m=m):
                mlp_rows(m * ROW_BLK)


def _experts(item_e, item_rows, n_active, xs, w_gate_up, b_gate_up, w_down, b_down, *, w_max):
    n_e, d, f2 = w_gate_up.shape
    d_exp = f2 // 2
    n_f = d_exp // TF_EXP
    slot_block = (SLOT_ROWS,) + xs.shape[1:]

    def item(w, na):
        return jnp.minimum(w, na[0] - 1)

    def fcol(w, f, na):
        return jnp.where(w < na[0], f, n_f - 1)

    def slot_idx(w, f, ie, nr, na):
        return (item(w, na), 0, 0)

    def slot_in_idx(w, f, ie, nr, na):
        return (item(w + (f >= n_f // 2).astype(I32), na), 0, 0)

    grid_spec = pltpu.PrefetchScalarGridSpec(
        num_scalar_prefetch=3,
        grid=(w_max, n_f),
        in_specs=[
            pl.BlockSpec(slot_block, slot_in_idx),
            pl.BlockSpec((1, d, TF_EXP), lambda w, f, ie, nr, na: (ie[item(w, na)], 0, fcol(w, f, na))),
            pl.BlockSpec((1, d, TF_EXP),
                         lambda w, f, ie, nr, na: (ie[item(w, na)], 0, n_f + fcol(w, f, na))),
            pl.BlockSpec((1, 1, TF_EXP), lambda w, f, ie, nr, na: (ie[item(w, na)], 0, fcol(w, f, na))),
            pl.BlockSpec((1, 1, TF_EXP),
                         lambda w, f, ie, nr, na: (ie[item(w, na)], 0, n_f + fcol(w, f, na))),
            pl.BlockSpec((1, TF_EXP, d), lambda w, f, ie, nr, na: (ie[item(w, na)], fcol(w, f, na), 0)),
            pl.BlockSpec((1, 1, d), lambda w, f, ie, nr, na: (ie[item(w, na)], 0, 0)),
        ],
        out_specs=pl.BlockSpec(slot_block, slot_idx),
        scratch_shapes=[
            pltpu.VMEM((SLOT_ROWS, d), BF16),
            pltpu.VMEM((SLOT_ROWS, d), F32),
        ],
    )
    bgu = b_gate_up.reshape(n_e, 1, f2)
    return pl.pallas_call(
        functools.partial(_expert_body, n_f=n_f),
        grid_spec=grid_spec,
        out_shape=jax.ShapeDtypeStruct(xs.shape, BF16),
        compiler_params=_params(2),
        name="experts",
    )(item_e, item_rows, n_active, xs, w_gate_up, w_gate_up, bgu, bgu, w_down,
      b_down.reshape(n_e, 1, d))


def _combine_body(pos_ref, pos_next_ref, gt_ref, h_ref, g_ref, b_ref, ys_ref, o_ref, buf_ref, sem,
                  *, alpha):
    i = pl.program_id(0)
    slot = i % 2

    def gather(p_ref, s):
        def issue(t, carry):
            for k in range(TOP_K):
                pltpu.make_async_copy(
                    ys_ref.at[p_ref[k, t]], buf_ref.at[s, k, t], sem.at[s]).start(priority=k % 2)
            return carry

        lax.fori_loop(0, TB_COMB, issue, 0, unroll=4)

    @pl.when(i == 0)
    def _():
        gather(pos_ref, 0)

    @pl.when(i + 1 < pl.num_programs(0))
    def _():
        gather(pos_next_ref, 1 - slot)

    for k in range(TOP_K):
        pltpu.make_async_copy(
            ys_ref.at[pl.ds(0, TB_COMB)], buf_ref.at[slot, k], sem.at[slot]).wait()

    ffn = jnp.zeros(h_ref.shape, F32)
    for k in range(TOP_K):
        ffn = ffn + gt_ref[:, k:k + 1] * _from_row_tiles(buf_ref[slot, k]).astype(F32)
    y = alpha * h_ref[...] + ffn
    mu = jnp.mean(y, axis=-1, keepdims=True)
    dev = y - mu
    var = jnp.mean(dev * dev, axis=-1, keepdims=True)
    o_ref[...] = dev * lax.rsqrt(var + LN_EPS) * g_ref[...] + b_ref[...]


def _combine(pos, gates_t, h1, ln_g, ln_b, ys, *, alpha):
    n_tok, d = h1.shape
    const = lambda i: (0, 0)
    n_tiles = n_tok // TB_COMB
    return pl.pallas_call(
        functools.partial(_combine_body, alpha=alpha),
        grid=(n_tiles,),
        in_specs=[
            pl.BlockSpec((TOP_K, TB_COMB), lambda i: (0, i), memory_space=pltpu.SMEM),
            pl.BlockSpec((TOP_K, TB_COMB), lambda i: (0, jnp.minimum(i + 1, n_tiles - 1)),
                         memory_space=pltpu.SMEM),
            pl.BlockSpec((TB_COMB, TOP_K), lambda i: (i, 0)),
            pl.BlockSpec((TB_COMB, d), lambda i: (i, 0)),
            pl.BlockSpec((1, d), const),
            pl.BlockSpec((1, d), const),
            pl.BlockSpec(memory_space=pl.ANY),
        ],
        out_specs=pl.BlockSpec((TB_COMB, d), lambda i: (i, 0)),
        out_shape=jax.ShapeDtypeStruct((n_tok, d), F32),
        scratch_shapes=[pltpu.VMEM((2, TOP_K, TB_COMB) + ys.shape[1:], BF16),
                        pltpu.SemaphoreType.DMA((2,))],
        compiler_params=_params(1),
        name="combine",
    )(pos, pos, gates_t, h1, ln_g.reshape(1, -1), ln_b.reshape(1, -1), ys)


def _rope_tables(seq, half):
    inv_freq = ROPE_BASE ** (-jnp.arange(half, dtype=F32) / half)
    ang = jnp.arange(seq, dtype=F32)[:, None] * inv_freq[None, :]
    return jnp.cos(ang), jnp.sin(ang)


def _layer(h, p, *, batch, seq, alpha, cos, sin):
    n_tok, d = h.shape
    conv_w = p["conv_w"].shape[1]
    ret_w = d - conv_w
    a, q, k, v, g = _in_proj(h, p["w_in"], p["b_in"], cos, sin, p["conv_w"], p["conv_b"],
                             p["conv_ln_g"], p["conv_ln_b"], seq=seq, conv_w=conv_w, ret_w=ret_w)
    h1, h1_packed, logits_t = _out_proj(a, q, k, v, g, p["w_out"], p["b_out"], h, p["ln1_g"],
                                        p["ln1_b"], p["w_router"], p["b_router"], alpha=alpha, seq=seq)
    pos, gates, meta = _route(logits_t)
    w_max = N_EXPERTS + (n_tok * TOP_K) // SLOT_ROWS
    xs = _dispatch(pos, h1_packed, w_max * SLOT_ROWS)
    ys = _experts(meta[0, :w_max], meta[1, :w_max], meta[2, :1], xs,
                  p["w_gate_up"], p["b_gate_up"], p["w_down"], p["b_down"], w_max=w_max)
    return _combine(pos, gates.T, h1, p["ln2_g"], p["ln2_b"], ys, alpha=alpha)


def kernel(x, w_in, b_in, conv_w, conv_b, conv_ln_g, conv_ln_b, w_out, b_out, ln1_g, ln1_b,
           w_router, b_router, w_gate_up, b_gate_up, w_down, b_down, ln2_g, ln2_b):
    batch, seq, d = x.shape
    depth = w_in.shape[0]
    alpha = float((2 * depth) ** 0.25)
    head_dim = (d - conv_w.shape[2]) // RET_HEADS
    cos, sin = _rope_tables(seq, head_dim // 2)
    stacked = dict(w_in=w_in, b_in=b_in, conv_w=conv_w, conv_b=conv_b, conv_ln_g=conv_ln_g,
                   conv_ln_b=conv_ln_b, w_out=w_out, b_out=b_out, ln1_g=ln1_g, ln1_b=ln1_b,
                   w_router=w_router, b_router=b_router, w_gate_up=w_gate_up, b_gate_up=b_gate_up,
                   w_down=w_down, b_down=b_down, ln2_g=ln2_g, ln2_b=ln2_b)
    h = x.reshape(batch * seq, d)
    for layer in range(depth):
        p = {name: val[layer] for name, val in stacked.items()}
        h = _layer(h, p, batch=batch, seq=seq, alpha=alpha, cos=cos, sin=sin)
    return h.reshape(batch, seq, d)
```

```python
import functools
import math

import jax
import jax.numpy as jnp
from jax import lax
from jax.experimental import pallas as pl
from jax.experimental.pallas import tpu as pltpu

F32 = jnp.float32
BF16 = jnp.bfloat16
I32 = jnp.int32

RET_HEADS = 4
CONV_KERNEL = 31
ROPE_BASE = 10000.0
N_EXPERTS = 32
TOP_K = 4
SWIGLU_ALPHA = 1.702
SWIGLU_LIMIT = 7.0
LN_EPS = 1e-5
LOG_GAMMA = tuple(math.log(1.0 - 2.0 ** (-5.0 - h)) for h in range(RET_HEADS))

LANES = 128
SUBLANES = 8
VMEM_LIMIT = 58 * 1024 * 1024

TM_IN = 1024
TN_IN = 512
CONV_PIECE = 128
HALO = 32
RC_CONV = 32
CH_RET = 256
TM_OUT = 512
PREFIX_BLK = 256
SLOT_ROWS = 1280
ROW_BLK = 128
FINE_ROWS = 768
TF_EXP = 256
TB_DISP = 512
TB_COMB = 256


def _params(n_axes):
    return pltpu.CompilerParams(
        dimension_semantics=("arbitrary",) * n_axes, vmem_limit_bytes=VMEM_LIMIT)


def _sigmoid(x):
    return 1.0 / (1.0 + jnp.exp(-x))


def _to_row_tiles(v):
    return v.astype(BF16).reshape(v.shape[0], v.shape[1] // LANES, LANES)


def _from_row_tiles(t):
    return t.reshape(t.shape[0], t.shape[1] * t.shape[2])


def _in_proj_body(x_ref, wa_ref, wb_ref, ba_ref, bb_ref, cos_ref, sin_ref, cw_ref, cb_ref, cg_ref, cbt_ref,
                  a_ref, q_ref, k_ref, v_ref, g_ref, xb_ref, u_ref, halo_ref, ext_ref, sh_ref,
                  *, n_a, n_r, head_dim, seq_tiles):
    i = pl.program_id(0)
    j = pl.program_id(1)
    width = u_ref.shape[1]

    @pl.when(j == 0)
    def _():
        xb_ref[...] = x_ref[...].astype(BF16)
        halo_ref[...] = u_ref[TM_IN - HALO:, :]

    pieces = [slice(c, c + head_dim) for c in range(0, TN_IN, head_dim)]

    def project(w_ref, b_ref, cols):
        return jnp.dot(xb_ref[...], w_ref[:, cols], preferred_element_type=F32) + b_ref[:, cols]

    for ja in range(n_a):
        @pl.when(j == ja)
        def _(ja=ja):
            for cols in pieces:
                val = project(wa_ref, ba_ref, cols)
                gate = project(wb_ref, bb_ref, cols)
                u_ref[:, ja * TN_IN + cols.start:ja * TN_IN + cols.stop] = (
                    val * _sigmoid(gate)).astype(BF16)

    def conv_piece():
        p = j - n_a
        r0 = pl.multiple_of(p * CONV_PIECE, CONV_PIECE)
        above = u_ref[pl.ds(pl.multiple_of(jnp.maximum(r0 - HALO, 0), HALO), HALO), :].astype(F32)
        halo = jnp.where(i % seq_tiles == 0, 0.0, halo_ref[...].astype(F32))
        ext_ref[0:HALO, :] = jnp.where(p == 0, halo, above)
        ext_ref[HALO:, :] = u_ref[pl.ds(r0, CONV_PIECE), :].astype(F32)
        first = HALO - (CONV_KERNEL - 1)
        ext_rows = ext_ref.shape[0]
        for s in range(SUBLANES):
            sh_ref[s, 0:ext_rows - SUBLANES, :] = ext_ref[s:s + ext_rows - SUBLANES, :]
        sh_ref[0, ext_rows - SUBLANES:, :] = ext_ref[ext_rows - SUBLANES:, :]
        for c0 in range(0, CONV_PIECE, RC_CONV):
            acc = jnp.broadcast_to(cb_ref[...], (RC_CONV, width))
            for tap in range(CONV_KERNEL):
                off = c0 + first + tap
                lo = off - off % SUBLANES
                acc = acc + cw_ref[tap:tap + 1, :] * sh_ref[off % SUBLANES, lo:lo + RC_CONV, :]
            mu = jnp.mean(acc, axis=-1, keepdims=True)
            dev = acc - mu
            var = jnp.mean(dev * dev, axis=-1, keepdims=True)
            y = dev * lax.rsqrt(var + LN_EPS) * cg_ref[...] + cbt_ref[...]
            a_ref[pl.ds(r0 + c0, RC_CONV), :] = (y * _sigmoid(y)).astype(BF16)

    def rotary(o_ref, scale):
        half = head_dim // 2
        for cols in pieces:
            acc = project(wa_ref, ba_ref, cols)
            t1 = acc[:, :half]
            t2 = acc[:, half:]
            cos = cos_ref[...]
            sin = sin_ref[...]
            o_ref[:, cols.start:cols.start + half] = ((t1 * cos - t2 * sin) * scale).astype(BF16)
            o_ref[:, cols.start + half:cols.stop] = ((t1 * sin + t2 * cos) * scale).astype(BF16)

    @pl.when((j >= n_a) & (j < n_a + n_r))
    def _():
        conv_piece()
        rotary(q_ref, 1.0)

    @pl.when((j >= n_a + n_r) & (j < n_a + 2 * n_r))
    def _():
        conv_piece()
        rotary(k_ref, head_dim ** -0.5)

    @pl.when((j >= n_a + 2 * n_r) & (j < n_a + 3 * n_r))
    def _():
        conv_piece()
        for cols in pieces:
            v_ref[:, cols] = project(wa_ref, ba_ref, cols).astype(BF16)

    @pl.when(j >= n_a + 3 * n_r)
    def _():
        conv_piece()
        for cols in pieces:
            acc = project(wa_ref, ba_ref, cols)
            g_ref[:, cols] = (acc * _sigmoid(acc)).astype(BF16)


def _in_proj(x2, w_in, b_in, cos, sin, conv_taps, conv_b, conv_g, conv_bt, *, seq, conv_w, ret_w):
    n, d = x2.shape
    head_dim = ret_w // RET_HEADS
    n_a = conv_w // TN_IN
    n_r = ret_w // TN_IN
    n_j = n_a + 4 * n_r
    assert CONV_PIECE * 4 * n_r == TM_IN and seq % TM_IN == 0
    seq_tiles = seq // TM_IN
    b2 = b_in.reshape(1, -1)
    w_bf = w_in.astype(BF16)

    def wa_idx(i, j):
        return (0, jnp.where(j < n_a, j, j + n_a))

    def wb_idx(i, j):
        return (0, jnp.where(j < n_a, n_a + j, 2 * n_a - 1))

    def seg(lo, cnt):
        return lambda i, j: (i, jnp.clip(j - lo, 0, cnt - 1))

    const = lambda i, j: (0, 0)
    out_a = jax.ShapeDtypeStruct((n, conv_w), BF16)
    out_r = jax.ShapeDtypeStruct((n, ret_w), BF16)
    return pl.pallas_call(
        functools.partial(_in_proj_body, n_a=n_a, n_r=n_r, head_dim=head_dim, seq_tiles=seq_tiles),
        grid=(n // TM_IN, n_j),
        in_specs=[
            pl.BlockSpec((TM_IN, d), lambda i, j: (i, 0)),
            pl.BlockSpec((d, TN_IN), wa_idx),
            pl.BlockSpec((d, TN_IN), wb_idx),
            pl.BlockSpec((1, TN_IN), wa_idx),
            pl.BlockSpec((1, TN_IN), wb_idx),
            pl.BlockSpec((TM_IN, head_dim // 2), lambda i, j: (i % seq_tiles, 0)),
            pl.BlockSpec((TM_IN, head_dim // 2), lambda i, j: (i % seq_tiles, 0)),
            pl.BlockSpec((CONV_KERNEL, conv_w), const),
            pl.BlockSpec((1, conv_w), const),
            pl.BlockSpec((1, conv_w), const),
            pl.BlockSpec((1, conv_w), const),
        ],
        out_specs=[
            pl.BlockSpec((TM_IN, conv_w), lambda i, j: (i, 0)),
            pl.BlockSpec((TM_IN, TN_IN), seg(n_a, n_r)),
            pl.BlockSpec((TM_IN, TN_IN), seg(n_a + n_r, n_r)),
            pl.BlockSpec((TM_IN, TN_IN), seg(n_a + 2 * n_r, n_r)),
            pl.BlockSpec((TM_IN, TN_IN), seg(n_a + 3 * n_r, n_r)),
        ],
        out_shape=[out_a, out_r, out_r, out_r, out_r],
        scratch_shapes=[pltpu.VMEM((TM_IN, d), BF16),
                        pltpu.VMEM((TM_IN, conv_w), BF16),
                        pltpu.VMEM((HALO, conv_w), BF16),
                        pltpu.VMEM((HALO + CONV_PIECE, conv_w), F32),
                        pltpu.VMEM((SUBLANES, HALO + CONV_PIECE, conv_w), F32)],
        compiler_params=_params(2),
        name="in_proj",
    )(x2, w_bf, w_bf, b2, b2, cos, sin, conv_taps, conv_b.reshape(1, -1), conv_g.reshape(1, -1),
      conv_bt.reshape(1, -1))


def _retention_mask_init(mask_ref):
    rel = (lax.broadcasted_iota(I32, (CH_RET, CH_RET), 0)
           - lax.broadcasted_iota(I32, (CH_RET, CH_RET), 1)).astype(F32)
    for h in range(RET_HEADS):
        mask_ref[h] = jnp.where(rel >= 0, jnp.exp(LOG_GAMMA[h] * jnp.maximum(rel, 0.0)), 0.0)


def _retention_chunk(q_ref, k_ref, v_ref, g_ref, r_ref, state_ref, mask_ref, rows, head_dim):
    idx = lax.broadcasted_iota(I32, (CH_RET, 1), 0).astype(F32)
    nt = (((1,), (1,)), ((), ()))
    tn = (((0,), (0,)), ((), ()))
    for h in range(RET_HEADS):
        sl = slice(h * head_dim, (h + 1) * head_dim)
        qh = q_ref[rows, sl]
        kh = k_ref[rows, sl]
        vh = v_ref[rows, sl]
        scores = lax.dot_general(qh, kh, nt, preferred_element_type=F32) * mask_ref[h]
        inner = jnp.dot(scores.astype(BF16), vh, preferred_element_type=F32)
        st = state_ref[h]
        q_decay = jnp.exp(LOG_GAMMA[h] * (idx + 1.0))
        cross = jnp.dot(qh, st.astype(BF16), preferred_element_type=F32) * q_decay
        k_decay = jnp.exp(LOG_GAMMA[h] * (CH_RET - 1.0 - idx))
        k_dec = (kh.astype(F32) * k_decay).astype(BF16)
        kv = lax.dot_general(k_dec, vh, tn, preferred_element_type=F32)
        state_ref[h] = math.exp(LOG_GAMMA[h] * CH_RET) * st + kv
        o = inner + cross
        mu = jnp.mean(o, axis=-1, keepdims=True)
        dev = o - mu
        var = jnp.mean(dev * dev, axis=-1, keepdims=True)
        r_ref[rows, sl] = (g_ref[rows, sl].astype(F32) * (dev * lax.rsqrt(var + LN_EPS))).astype(BF16)


def _out_proj_body(a_ref, q_ref, k_ref, v_ref, gate_ref, wt_ref, wb_ref, bo_ref, x_ref, g_ref, b_ref,
                   wr_ref, br_ref, h_ref, hp_ref, lt_ref, r_ref, state_ref, mask_ref,
                   *, alpha, head_dim, seq_tiles):
    i = pl.program_id(0)

    @pl.when(i == 0)
    def _():
        _retention_mask_init(mask_ref)

    @pl.when(i % seq_tiles == 0)
    def _():
        state_ref[...] = jnp.zeros_like(state_ref)

    for c0 in range(0, TM_OUT, CH_RET):
        _retention_chunk(q_ref, k_ref, v_ref, gate_ref, r_ref, state_ref, mask_ref,
                         slice(c0, c0 + CH_RET), head_dim)
    mix = (jnp.dot(a_ref[...], wt_ref[...], preferred_element_type=F32)
           + jnp.dot(r_ref[...], wb_ref[...], preferred_element_type=F32) + bo_ref[...])
    y = alpha * x_ref[...] + mix
    mu = jnp.mean(y, axis=-1, keepdims=True)
    dev = y - mu
    var = jnp.mean(dev * dev, axis=-1, keepdims=True)
    h1 = dev * lax.rsqrt(var + LN_EPS) * g_ref[...] + b_ref[...]
    h_ref[...] = h1
    hp_ref[...] = _to_row_tiles(h1)
    h_hi = h1.astype(BF16)
    h_lo = (h1 - h_hi.astype(F32)).astype(BF16)
    wr = wr_ref[...]
    w_hi = wr.astype(BF16)
    w_lo = (wr - w_hi.astype(F32)).astype(BF16)
    hi_both = jnp.dot(h_hi, jnp.concatenate([w_hi, w_lo], axis=1), preferred_element_type=F32)
    logits = (hi_both[:, :LANES] + hi_both[:, LANES:]
              + jnp.dot(h_lo, w_hi, preferred_element_type=F32) + br_ref[...])
    lt_ref[...] = logits.T[:N_EXPERTS, :]


def _out_proj(a, q, k, v, gate, w_out, b_out, x2, ln_g, ln_b, w_router, b_router, *, alpha, seq):
    n, d = x2.shape
    cw = a.shape[1]
    rw = q.shape[1]
    head_dim = rw // RET_HEADS
    assert seq % TM_OUT == 0 and TM_OUT % CH_RET == 0
    w_bf = w_out.astype(BF16)
    wr_pad = jnp.pad(w_router, ((0, 0), (0, LANES - N_EXPERTS)))
    br_pad = jnp.pad(b_router, (0, LANES - N_EXPERTS)).reshape(1, LANES)
    const = lambda i: (0, 0)
    ret_spec = pl.BlockSpec((TM_OUT, rw), lambda i: (i, 0))
    return pl.pallas_call(
        functools.partial(_out_proj_body, alpha=alpha, head_dim=head_dim, seq_tiles=seq // TM_OUT),
        grid=(n // TM_OUT,),
        in_specs=[
            pl.BlockSpec((TM_OUT, cw), lambda i: (i, 0)),
            ret_spec, ret_spec, ret_spec, ret_spec,
            pl.BlockSpec((cw, d), const),
            pl.BlockSpec((rw, d), lambda i: (cw // rw, 0)),
            pl.BlockSpec((1, d), const),
            pl.BlockSpec((TM_OUT, d), lambda i: (i, 0)),
            pl.BlockSpec((1, d), const),
            pl.BlockSpec((1, d), const),
            pl.BlockSpec((d, LANES), const),
            pl.BlockSpec((1, LANES), const),
        ],
        out_specs=[
            pl.BlockSpec((TM_OUT, d), lambda i: (i, 0)),
            pl.BlockSpec((TM_OUT, d // LANES, LANES), lambda i: (i, 0, 0)),
            pl.BlockSpec((N_EXPERTS, TM_OUT), lambda i: (0, i)),
        ],
        out_shape=[jax.ShapeDtypeStruct((n, d), F32),
                   jax.ShapeDtypeStruct((n, d // LANES, LANES), BF16),
                   jax.ShapeDtypeStruct((N_EXPERTS, n), F32)],
        scratch_shapes=[pltpu.VMEM((TM_OUT, rw), BF16),
                        pltpu.VMEM((RET_HEADS, head_dim, head_dim), F32),
                        pltpu.VMEM((RET_HEADS, CH_RET, CH_RET), F32)],
        compiler_params=_params(1),
        name="out_proj",
    )(a, q, k, v, gate, w_bf, w_bf, b_out.reshape(1, -1), x2, ln_g.reshape(1, -1),
      ln_b.reshape(1, -1), wr_pad, br_pad)


def _route_body(lt_ref, pos_ref, gate_ref, meta_ref, rank_ref):
    n_tok = lt_ref.shape[1]
    slot = float(SLOT_ROWS)
    l = lt_ref[...]
    eio = lax.broadcasted_iota(I32, (N_EXPERTS, n_tok), 0).astype(F32)
    vals, idxs = [], []
    for _ in range(TOP_K):
        m = jnp.max(l, axis=0, keepdims=True)
        idx = jnp.min(jnp.where(l == m, eio, float(N_EXPERTS)), axis=0, keepdims=True)
        vals.append(m)
        idxs.append(idx)
        l = jnp.where(eio == idx, -jnp.inf, l)
    exps = [jnp.exp(v - vals[0]) for v in vals]
    den = exps[0] + exps[1] + exps[2] + exps[3]
    for k in range(TOP_K):
        gate_ref[k:k + 1, :] = exps[k] / den

    chosen = jnp.zeros((N_EXPERTS, n_tok), F32)
    for k in range(TOP_K):
        chosen = jnp.where(eio == idxs[k], 1.0, chosen)
    tri = (lax.broadcasted_iota(I32, (PREFIX_BLK, PREFIX_BLK), 0)
           < lax.broadcasted_iota(I32, (PREFIX_BLK, PREFIX_BLK), 1)).astype(BF16)
    count = jnp.zeros((N_EXPERTS, 1), F32)
    for blk in range(n_tok // PREFIX_BLK):
        cb = chosen[:, blk * PREFIX_BLK:(blk + 1) * PREFIX_BLK]
        pre = jnp.dot(cb.astype(BF16), tri, preferred_element_type=F32)
        rank_ref[:, blk * PREFIX_BLK:(blk + 1) * PREFIX_BLK] = pre + count
        count = count + jnp.sum(cb, axis=1, keepdims=True)

    n_items = jnp.broadcast_to(jnp.floor((count + (slot - 1.0)) / slot), (N_EXPERTS, LANES))
    rio = lax.broadcasted_iota(I32, (N_EXPERTS, LANES), 0)
    inc = n_items
    shift = 1
    while shift < N_EXPERTS:
        inc = inc + jnp.where(rio >= shift, pltpu.roll(inc, shift, axis=0), 0.0)
        shift *= 2
    start = inc - n_items

    rank = rank_ref[...]
    rq = jnp.floor((rank + 0.5) / slot)
    dest = (start[:, :1] + rq) * slot + (rank - rq * slot)
    for k in range(TOP_K):
        pos_ref[k:k + 1, :] = jnp.sum(
            jnp.where(eio == idxs[k], dest, 0.0), axis=0, keepdims=True).astype(I32)

    wio = lax.broadcasted_iota(I32, (N_EXPERTS, LANES), 1).astype(F32)
    owns = (wio >= start) & (wio < start + n_items)
    rows = jnp.clip(count - (wio - start) * slot, 0.0, slot)
    item_e = jnp.sum(jnp.where(owns, rio.astype(F32), 0.0), axis=0, keepdims=True)
    item_rows = jnp.sum(jnp.where(owns, rows, 0.0), axis=0, keepdims=True)
    meta_ref[...] = jnp.zeros_like(meta_ref)
    meta_ref[0:1, :] = item_e.astype(I32)
    meta_ref[1:2, :] = item_rows.astype(I32)
    meta_ref[2:3, :] = inc[N_EXPERTS - 1:N_EXPERTS, :].astype(I32)


def _route(lt):
    n_tok = lt.shape[1]
    return pl.pallas_call(
        _route_body,
        out_shape=[jax.ShapeDtypeStruct((TOP_K, n_tok), I32),
                   jax.ShapeDtypeStruct((TOP_K, n_tok), F32),
                   jax.ShapeDtypeStruct((8, LANES), I32)],
        scratch_shapes=[pltpu.VMEM((N_EXPERTS, n_tok), F32)],
        compiler_params=pltpu.CompilerParams(vmem_limit_bytes=VMEM_LIMIT),
        name="route",
    )(lt)


def _dispatch_body(pos_ref, h_ref, xs_ref, sem):
    def row_copy(t, k):
        return pltpu.make_async_copy(h_ref.at[t], xs_ref.at[pos_ref[k, t]], sem)

    def issue(t, carry):
        for k in range(TOP_K):
            row_copy(t, k).start(priority=k % 2)
        return carry

    lax.fori_loop(0, TB_DISP, issue, 0, unroll=4)
    for _ in range(TOP_K):
        pltpu.make_async_copy(h_ref, xs_ref.at[pl.ds(0, TB_DISP)], sem).wait()


def _dispatch(pos, hp, n_rows):
    n_tok, subl, lanes = hp.shape
    return pl.pallas_call(
        _dispatch_body,
        grid=(n_tok // TB_DISP,),
        in_specs=[
            pl.BlockSpec((TOP_K, TB_DISP), lambda i: (0, i), memory_space=pltpu.SMEM),
            pl.BlockSpec((TB_DISP, subl, lanes), lambda i: (i, 0, 0)),
        ],
        out_specs=pl.BlockSpec(memory_space=pl.ANY),
        out_shape=jax.ShapeDtypeStruct((n_rows, subl, lanes), BF16),
        scratch_shapes=[pltpu.SemaphoreType.DMA],
        compiler_params=_params(1),
        name="dispatch",
    )(pos, hp)


def _expert_body(ie_ref, nr_ref, na_ref, xs_ref, wg_ref, wl_ref, bg_ref, bl_ref, wd_ref, bd_ref,
                 ys_ref, xb_ref, acc_ref, *, n_f):
    w = pl.program_id(0)
    f = pl.program_id(1)

    @pl.when(w < na_ref[0])
    def _():
        n_rows = nr_ref[w]

        @pl.when(f == 0)
        def _():
            xb_ref[...] = _from_row_tiles(xs_ref[...])
            last0 = ((n_rows + (ROW_BLK - 1)) // ROW_BLK - 1) * ROW_BLK
            last = pl.ds(pl.multiple_of(last0, ROW_BLK), ROW_BLK)
            row = lax.broadcasted_iota(I32, (ROW_BLK, xb_ref.shape[1]), 0) + last0
            xb_ref[last, :] = jnp.where(row < n_rows, xb_ref[last, :], jnp.zeros((), BF16))

        bg = bg_ref[0]
        bl = bl_ref[0]

        def mlp_rows(n):
            xblk = xb_ref[0:n, :]
            hg = jnp.dot(xblk, wg_ref[0].astype(BF16), preferred_element_type=F32) + bg
            hl = jnp.dot(xblk, wl_ref[0].astype(BF16), preferred_element_type=F32) + bl
            x_glu = jnp.minimum(hg, SWIGLU_LIMIT)
            x_lin = jnp.clip(hl, -SWIGLU_LIMIT, SWIGLU_LIMIT)
            act = x_glu * _sigmoid(SWIGLU_ALPHA * x_glu) * (x_lin + 1.0)
            start = jnp.where(f == 0, bd_ref[0], acc_ref[0:n, :])
            acc_ref[0:n, :] = start + jnp.dot(
                act.astype(BF16), wd_ref[0].astype(BF16), preferred_element_type=F32)

            @pl.when(f == n_f - 1)
            def _():
                ys_ref[0:n] = _to_row_tiles(acc_ref[0:n, :])

        fine = ROW_BLK // 2
        live_coarse = (n_rows + (ROW_BLK - 1)) // ROW_BLK * ROW_BLK
        live_fine = (n_rows + (fine - 1)) // fine * fine
        n_live = jnp.where(live_fine > FINE_ROWS, live_fine, live_coarse)
        variants = list(range(ROW_BLK, FINE_ROWS, ROW_BLK)) + list(range(FINE_ROWS, SLOT_ROWS + 1, fine))
        for rows in variants:
            @pl.when(n_live == rows)
            def _(rows=rows):
                mlp_rows(rows)


def _experts(item_e, item_rows, n_active, xs, w_gate_up, b_gate_up, w_down, b_down, *, w_max):
    n_e, d, f2 = w_gate_up.shape
    d_exp = f2 // 2
    n_f = d_exp // TF_EXP
    slot_block = (SLOT_ROWS,) + xs.shape[1:]

    def item(w, na):
        return jnp.minimum(w, na[0] - 1)

    def fcol(w, f, na):
        return jnp.where(w < na[0], f, n_f - 1)

    def slot_idx(w, f, ie, nr, na):
        return (item(w, na), 0, 0)

    def slot_in_idx(w, f, ie, nr, na):
        return (item(w + (f >= n_f // 2).astype(I32), na), 0, 0)

    grid_spec = pltpu.PrefetchScalarGridSpec(
        num_scalar_prefetch=3,
        grid=(w_max, n_f),
        in_specs=[
            pl.BlockSpec(slot_block, slot_in_idx),
            pl.BlockSpec((1, d, TF_EXP), lambda w, f, ie, nr, na: (ie[item(w, na)], 0, fcol(w, f, na))),
            pl.BlockSpec((1, d, TF_EXP),
                         lambda w, f, ie, nr, na: (ie[item(w, na)], 0, n_f + fcol(w, f, na))),
            pl.BlockSpec((1, 1, TF_EXP), lambda w, f, ie, nr, na: (ie[item(w, na)], 0, fcol(w, f, na))),
            pl.BlockSpec((1, 1, TF_EXP),
                         lambda w, f, ie, nr, na: (ie[item(w, na)], 0, n_f + fcol(w, f, na))),
            pl.BlockSpec((1, TF_EXP, d), lambda w, f, ie, nr, na: (ie[item(w, na)], fcol(w, f, na), 0)),
            pl.BlockSpec((1, 1, d), lambda w, f, ie, nr, na: (ie[item(w, na)], 0, 0)),
        ],
        out_specs=pl.BlockSpec(slot_block, slot_idx),
        scratch_shapes=[
            pltpu.VMEM((SLOT_ROWS, d), BF16),
            pltpu.VMEM((SLOT_ROWS, d), F32),
        ],
    )
    bgu = b_gate_up.reshape(n_e, 1, f2)
    return pl.pallas_call(
        functools.partial(_expert_body, n_f=n_f),
        grid_spec=grid_spec,
        out_shape=jax.ShapeDtypeStruct(xs.shape, BF16),
        compiler_params=_params(2),
        name="experts",
    )(item_e, item_rows, n_active, xs, w_gate_up, w_gate_up, bgu, bgu, w_down,
      b_down.reshape(n_e, 1, d))


def _combine_body(pos_ref, pos_next_ref, gt_ref, h_ref, g_ref, b_ref, ys_ref, o_ref, buf_ref, sem,
                  *, alpha):
    i = pl.program_id(0)
    slot = i % 2

    def gather(p_ref, s):
        def issue(t, carry):
            for k in range(TOP_K):
                pltpu.make_async_copy(
                    ys_ref.at[p_ref[k, t]], buf_ref.at[s, k, t], sem.at[s]).start(priority=k % 2)
            return carry

        lax.fori_loop(0, TB_COMB, issue, 0, unroll=4)

    @pl.when(i == 0)
    def _():
        gather(pos_ref, 0)

    @pl.when(i + 1 < pl.num_programs(0))
    def _():
        gather(pos_next_ref, 1 - slot)

    for k in range(TOP_K):
        pltpu.make_async_copy(
            ys_ref.at[pl.ds(0, TB_COMB)], buf_ref.at[slot, k], sem.at[slot]).wait()

    ffn = jnp.zeros(h_ref.shape, F32)
    for k in range(TOP_K):
        ffn = ffn + gt_ref[:, k:k + 1] * _from_row_tiles(buf_ref[slot, k]).astype(F32)
    y = alpha * h_ref[...] + ffn
    mu = jnp.mean(y, axis=-1, keepdims=True)
    dev = y - mu
    var = jnp.mean(dev * dev, axis=-1, keepdims=True)
    o_ref[...] = dev * lax.rsqrt(var + LN_EPS) * g_ref[...] + b_ref[...]


def _combine(pos, gates_t, h1, ln_g, ln_b, ys, *, alpha):
    n_tok, d = h1.shape
    const = lambda i: (0, 0)
    n_tiles = n_tok // TB_COMB
    return pl.pallas_call(
        functools.partial(_combine_body, alpha=alpha),
        grid=(n_tiles,),
        in_specs=[
            pl.BlockSpec((TOP_K, TB_COMB), lambda i: (0, i), memory_space=pltpu.SMEM),
            pl.BlockSpec((TOP_K, TB_COMB), lambda i: (0, jnp.minimum(i + 1, n_tiles - 1)),
                         memory_space=pltpu.SMEM),
            pl.BlockSpec((TB_COMB, TOP_K), lambda i: (i, 0)),
            pl.BlockSpec((TB_COMB, d), lambda i: (i, 0)),
            pl.BlockSpec((1, d), const),
            pl.BlockSpec((1, d), const),
            pl.BlockSpec(memory_space=pl.ANY),
        ],
        out_specs=pl.BlockSpec((TB_COMB, d), lambda i: (i, 0)),
        out_shape=jax.ShapeDtypeStruct((n_tok, d), F32),
        scratch_shapes=[pltpu.VMEM((2, TOP_K, TB_COMB) + ys.shape[1:], BF16),
                        pltpu.SemaphoreType.DMA((2,))],
        compiler_params=_params(1),
        name="combine",
    )(pos, pos, gates_t, h1, ln_g.reshape(1, -1), ln_b.reshape(1, -1), ys)


def _rope_tables(seq, half):
    inv_freq = ROPE_BASE ** (-jnp.arange(half, dtype=F32) / half)
    ang = jnp.arange(seq, dtype=F32)[:, None] * inv_freq[None, :]
    return jnp.cos(ang), jnp.sin(ang)


def _layer(h, p, *, batch, seq, alpha, cos, sin):
    n_tok, d = h.shape
    conv_w = p["conv_w"].shape[1]
    ret_w = d - conv_w
    a, q, k, v, g = _in_proj(h, p["w_in"], p["b_in"], cos, sin, p["conv_w"], p["conv_b"],
                             p["conv_ln_g"], p["conv_ln_b"], seq=seq, conv_w=conv_w, ret_w=ret_w)
    h1, h1_packed, logits_t = _out_proj(a, q, k, v, g, p["w_out"], p["b_out"], h, p["ln1_g"],
                                        p["ln1_b"], p["w_router"], p["b_router"], alpha=alpha, seq=seq)
    pos, gates, meta = _route(logits_t)
    w_max = N_EXPERTS + (n_tok * TOP_K) // SLOT_ROWS
    xs = _dispatch(pos, h1_packed, w_max * SLOT_ROWS)
    ys = _experts(meta[0, :w_max], meta[1, :w_max], meta[2, :1], xs,
                  p["w_gate_up"], p["b_gate_up"], p["w_down"], p["b_down"], w_max=w_max)
    return _combine(pos, gates.T, h1, p["ln2_g"], p["ln2_b"], ys, alpha=alpha)


def kernel(x, w_in, b_in, conv_w, conv_b, conv_ln_g, conv_ln_b, w_out, b_out, ln1_g, ln1_b,
           w_router, b_router, w_gate_up, b_gate_up, w_down, b_down, ln2_g, ln2_b):
    batch, seq, d = x.shape
    depth = w_in.shape[0]
    alpha = float((2 * depth) ** 0.25)
    head_dim = (d - conv_w.shape[2]) // RET_HEADS
    cos, sin = _rope_tables(seq, head_dim // 2)
    stacked = dict(w_in=w_in, b_in=b_in, conv_w=conv_w, conv_b=conv_b, conv_ln_g=conv_ln_g,
                   conv_ln_b=conv_ln_b, w_out=w_out, b_out=b_out, ln1_g=ln1_g, ln1_b=ln1_b,
                   w_router=w_router, b_router=b_router, w_gate_up=w_gate_up, b_gate_up=b_gate_up,
                   w_down=w_down, b_down=b_down, ln2_g=ln2_g, ln2_b=ln2_b)
    h = x.reshape(batch * seq, d)
    for layer in range(depth):
        p = {name: val[layer] for name, val in stacked.items()}
        h = _layer(h, p, batch=batch, seq=seq, alpha=alpha, cos=cos, sin=sin)
    return h.reshape(batch, seq, d)
```

```python
import functools
import math

import jax
import jax.numpy as jnp
from jax import lax
from jax.experimental import pallas as pl
from jax.experimental.pallas import tpu as pltpu

F32 = jnp.float32
BF16 = jnp.bfloat16
I32 = jnp.int32

RET_HEADS = 4
CONV_KERNEL = 31
ROPE_BASE = 10000.0
N_EXPERTS = 32
TOP_K = 4
SWIGLU_ALPHA = 1.702
SWIGLU_LIMIT = 7.0
LN_EPS = 1e-5
LOG_GAMMA = tuple(math.log(1.0 - 2.0 ** (-5.0 - h)) for h in range(RET_HEADS))

LANES = 128
SUBLANES = 8
VMEM_LIMIT = 58 * 1024 * 1024

TM_IN = 1024
TN_IN = 512
CONV_PIECE = 128
HALO = 32
RC_CONV = 128
CH_RET = 256
TM_OUT = 512
PREFIX_BLK = 256
SLOT_ROWS = 1280
ROW_BLK = 128
TF_EXP = 256
TB_DISP = 1024
TB_COMB = 128


def _params(n_axes):
    return pltpu.CompilerParams(
        dimension_semantics=("arbitrary",) * n_axes, vmem_limit_bytes=VMEM_LIMIT)


def _sigmoid(x):
    return 1.0 / (1.0 + jnp.exp(-x))


def _to_row_tiles(v):
    return v.astype(BF16).reshape(v.shape[0], v.shape[1] // LANES, LANES)


def _from_row_tiles(t):
    return t.reshape(t.shape[0], t.shape[1] * t.shape[2])


def _in_proj_body(x_ref, wa_ref, wb_ref, ba_ref, bb_ref, cos_ref, sin_ref, cw_ref, cb_ref, cg_ref, cbt_ref,
                  a_ref, q_ref, k_ref, v_ref, g_ref, xb_ref, u_ref, halo_ref, ext_ref, sh_ref,
                  *, n_a, n_r, head_dim, seq_tiles):
    i = pl.program_id(0)
    j = pl.program_id(1)
    width = u_ref.shape[1]

    @pl.when(j == 0)
    def _():
        xb_ref[...] = x_ref[...].astype(BF16)
        halo_ref[...] = u_ref[TM_IN - HALO:, :]

    pieces = [slice(c, c + head_dim) for c in range(0, TN_IN, head_dim)]

    def project(w_ref, b_ref, cols):
        return jnp.dot(xb_ref[...], w_ref[:, cols], preferred_element_type=F32) + b_ref[:, cols]

    for ja in range(n_a):
        @pl.when(j == ja)
        def _(ja=ja):
            for cols in pieces:
                val = project(wa_ref, ba_ref, cols)
                gate = project(wb_ref, bb_ref, cols)
                u_ref[:, ja * TN_IN + cols.start:ja * TN_IN + cols.stop] = (
                    val * _sigmoid(gate)).astype(BF16)

    def conv_piece():
        p = j - n_a
        r0 = pl.multiple_of(p * CONV_PIECE, CONV_PIECE)
        above = u_ref[pl.ds(pl.multiple_of(jnp.maximum(r0 - HALO, 0), HALO), HALO), :].astype(F32)
        halo = jnp.where(i % seq_tiles == 0, 0.0, halo_ref[...].astype(F32))
        ext_ref[0:HALO, :] = jnp.where(p == 0, halo, above)
        ext_ref[HALO:, :] = u_ref[pl.ds(r0, CONV_PIECE), :].astype(F32)
        first = HALO - (CONV_KERNEL - 1)
        ext_rows = ext_ref.shape[0]
        for s in range(SUBLANES):
            sh_ref[s, 0:ext_rows - SUBLANES, :] = ext_ref[s:s + ext_rows - SUBLANES, :]
        sh_ref[0, ext_rows - SUBLANES:, :] = ext_ref[ext_rows - SUBLANES:, :]
        for c0 in range(0, CONV_PIECE, RC_CONV):
            acc = jnp.broadcast_to(cb_ref[...], (RC_CONV, width))
            for tap in range(CONV_KERNEL):
                off = c0 + first + tap
                lo = off - off % SUBLANES
                acc = acc + cw_ref[tap:tap + 1, :] * sh_ref[off % SUBLANES, lo:lo + RC_CONV, :]
            mu = jnp.mean(acc, axis=-1, keepdims=True)
            dev = acc - mu
            var = jnp.mean(dev * dev, axis=-1, keepdims=True)
            y = dev * lax.rsqrt(var + LN_EPS) * cg_ref[...] + cbt_ref[...]
            a_ref[pl.ds(r0 + c0, RC_CONV), :] = (y * _sigmoid(y)).astype(BF16)

    def rotary(o_ref, scale):
        half = head_dim // 2
        for cols in pieces:
            acc = project(wa_ref, ba_ref, cols)
            t1 = acc[:, :half]
            t2 = acc[:, half:]
            cos = cos_ref[...]
            sin = sin_ref[...]
            o_ref[:, cols.start:cols.start + half] = ((t1 * cos - t2 * sin) * scale).astype(BF16)
            o_ref[:, cols.start + half:cols.stop] = ((t1 * sin + t2 * cos) * scale).astype(BF16)

    @pl.when((j >= n_a) & (j < n_a + n_r))
    def _():
        conv_piece()
        rotary(q_ref, 1.0)

    @pl.when((j >= n_a + n_r) & (j < n_a + 2 * n_r))
    def _():
        conv_piece()
        rotary(k_ref, head_dim ** -0.5)

    @pl.when((j >= n_a + 2 * n_r) & (j < n_a + 3 * n_r))
    def _():
        conv_piece()
        for cols in pieces:
            v_ref[:, cols] = project(wa_ref, ba_ref, cols).astype(BF16)

    @pl.when(j >= n_a + 3 * n_r)
    def _():
        conv_piece()
        for cols in pieces:
            acc = project(wa_ref, ba_ref, cols)
            g_ref[:, cols] = (acc * _sigmoid(acc)).astype(BF16)


def _in_proj(x2, w_in, b_in, cos, sin, conv_taps, conv_b, conv_g, conv_bt, *, seq, conv_w, ret_w):
    n, d = x2.shape
    head_dim = ret_w // RET_HEADS
    n_a = conv_w // TN_IN
    n_r = ret_w // TN_IN
    n_j = n_a + 4 * n_r
    assert CONV_PIECE * 4 * n_r == TM_IN and seq % TM_IN == 0
    seq_tiles = seq // TM_IN
    b2 = b_in.reshape(1, -1)
    w_bf = w_in.astype(BF16)

    def wa_idx(i, j):
        return (0, jnp.where(j < n_a, j, j + n_a))

    def wb_idx(i, j):
        return (0, jnp.where(j < n_a, n_a + j, 2 * n_a - 1))

    def seg(lo, cnt):
        return lambda i, j: (i, jnp.clip(j - lo, 0, cnt - 1))

    const = lambda i, j: (0, 0)
    out_a = jax.ShapeDtypeStruct((n, conv_w), BF16)
    out_r = jax.ShapeDtypeStruct((n, ret_w), BF16)
    return pl.pallas_call(
        functools.partial(_in_proj_body, n_a=n_a, n_r=n_r, head_dim=head_dim, seq_tiles=seq_tiles),
        grid=(n // TM_IN, n_j),
        in_specs=[
            pl.BlockSpec((TM_IN, d), lambda i, j: (i, 0)),
            pl.BlockSpec((d, TN_IN), wa_idx),
            pl.BlockSpec((d, TN_IN), wb_idx),
            pl.BlockSpec((1, TN_IN), wa_idx),
            pl.BlockSpec((1, TN_IN), wb_idx),
            pl.BlockSpec((TM_IN, head_dim // 2), lambda i, j: (i % seq_tiles, 0)),
            pl.BlockSpec((TM_IN, head_dim // 2), lambda i, j: (i % seq_tiles, 0)),
            pl.BlockSpec((CONV_KERNEL, conv_w), const),
            pl.BlockSpec((1, conv_w), const),
            pl.BlockSpec((1, conv_w), const),
            pl.BlockSpec((1, conv_w), const),
        ],
        out_specs=[
            pl.BlockSpec((TM_IN, conv_w), lambda i, j: (i, 0)),
            pl.BlockSpec((TM_IN, TN_IN), seg(n_a, n_r)),
            pl.BlockSpec((TM_IN, TN_IN), seg(n_a + n_r, n_r)),
            pl.BlockSpec((TM_IN, TN_IN), seg(n_a + 2 * n_r, n_r)),
            pl.BlockSpec((TM_IN, TN_IN), seg(n_a + 3 * n_r, n_r)),
        ],
        out_shape=[out_a, out_r, out_r, out_r, out_r],
        scratch_shapes=[pltpu.VMEM((TM_IN, d), BF16),
                        pltpu.VMEM((TM_IN, conv_w), BF16),
                        pltpu.VMEM((HALO, conv_w), BF16),
                        pltpu.VMEM((HALO + CONV_PIECE, conv_w), F32),
                        pltpu.VMEM((SUBLANES, HALO + CONV_PIECE, conv_w), F32)],
        compiler_params=_params(2),
        name="in_proj",
    )(x2, w_bf, w_bf, b2, b2, cos, sin, conv_taps, conv_b.reshape(1, -1), conv_g.reshape(1, -1),
      conv_bt.reshape(1, -1))


def _retention_mask_init(mask_ref):
    rel = (lax.broadcasted_iota(I32, (CH_RET, CH_RET), 0)
           - lax.broadcasted_iota(I32, (CH_RET, CH_RET), 1)).astype(F32)
    for h in range(RET_HEADS):
        mask_ref[h] = jnp.where(rel >= 0, jnp.exp(LOG_GAMMA[h] * jnp.maximum(rel, 0.0)), 0.0)


def _retention_chunk(q_ref, k_ref, v_ref, g_ref, r_ref, state_ref, mask_ref, rows, head_dim):
    idx = lax.broadcasted_iota(I32, (CH_RET, 1), 0).astype(F32)
    nt = (((1,), (1,)), ((), ()))
    tn = (((0,), (0,)), ((), ()))
    for h in range(RET_HEADS):
        sl = slice(h * head_dim, (h + 1) * head_dim)
        qh = q_ref[rows, sl]
        kh = k_ref[rows, sl]
        vh = v_ref[rows, sl]
        scores = lax.dot_general(qh, kh, nt, preferred_element_type=F32) * mask_ref[h]
        inner = jnp.dot(scores.astype(BF16), vh, preferred_element_type=F32)
        st = state_ref[h]
        q_decay = jnp.exp(LOG_GAMMA[h] * (idx + 1.0))
        cross = jnp.dot(qh, st.astype(BF16), preferred_element_type=F32) * q_decay
        k_decay = jnp.exp(LOG_GAMMA[h] * (CH_RET - 1.0 - idx))
        k_dec = (kh.astype(F32) * k_decay).astype(BF16)
        kv = lax.dot_general(k_dec, vh, tn, preferred_element_type=F32)
        state_ref[h] = math.exp(LOG_GAMMA[h] * CH_RET) * st + kv
        o = inner + cross
        mu = jnp.mean(o, axis=-1, keepdims=True)
        dev = o - mu
        var = jnp.mean(dev * dev, axis=-1, keepdims=True)
        r_ref[rows, sl] = (g_ref[rows, sl].astype(F32) * (dev * lax.rsqrt(var + LN_EPS))).astype(BF16)


def _out_proj_body(a_ref, q_ref, k_ref, v_ref, gate_ref, wt_ref, wb_ref, bo_ref, x_ref, g_ref, b_ref,
                   wr_ref, br_ref, h_ref, hp_ref, lt_ref, r_ref, state_ref, mask_ref,
                   *, alpha, head_dim, seq_tiles):
    i = pl.program_id(0)

    @pl.when(i == 0)
    def _():
        _retention_mask_init(mask_ref)

    @pl.when(i % seq_tiles == 0)
    def _():
        state_ref[...] = jnp.zeros_like(state_ref)

    for c0 in range(0, TM_OUT, CH_RET):
        _retention_chunk(q_ref, k_ref, v_ref, gate_ref, r_ref, state_ref, mask_ref,
                         slice(c0, c0 + CH_RET), head_dim)
    mix = (jnp.dot(a_ref[...], wt_ref[...], preferred_element_type=F32)
           + jnp.dot(r_ref[...], wb_ref[...], preferred_element_type=F32) + bo_ref[...])
    y = alpha * x_ref[...] + mix
    mu = jnp.mean(y, axis=-1, keepdims=True)
    dev = y - mu
    var = jnp.mean(dev * dev, axis=-1, keepdims=True)
    h1 = dev * lax.rsqrt(var + LN_EPS) * g_ref[...] + b_ref[...]
    h_ref[...] = h1
    hp_ref[...] = _to_row_tiles(h1)
    h_hi = h1.astype(BF16)
    h_lo = (h1 - h_hi.astype(F32)).astype(BF16)
    wr = wr_ref[...]
    w_hi = wr.astype(BF16)
    w_lo = (wr - w_hi.astype(F32)).astype(BF16)
    hi_both = jnp.dot(h_hi, jnp.concatenate([w_hi, w_lo], axis=1), preferred_element_type=F32)
    logits = (hi_both[:, :LANES] + hi_both[:, LANES:]
              + jnp.dot(h_lo, w_hi, preferred_element_type=F32) + br_ref[...])
    lt_ref[...] = logits.T[:N_EXPERTS, :]


def _out_proj(a, q, k, v, gate, w_out, b_out, x2, ln_g, ln_b, w_router, b_router, *, alpha, seq):
    n, d = x2.shape
    cw = a.shape[1]
    rw = q.shape[1]
    head_dim = rw // RET_HEADS
    assert seq % TM_OUT == 0 and TM_OUT % CH_RET == 0
    w_bf = w_out.astype(BF16)
    wr_pad = jnp.pad(w_router, ((0, 0), (0, LANES - N_EXPERTS)))
    br_pad = jnp.pad(b_router, (0, LANES - N_EXPERTS)).reshape(1, LANES)
    const = lambda i: (0, 0)
    ret_spec = pl.BlockSpec((TM_OUT, rw), lambda i: (i, 0))
    return pl.pallas_call(
        functools.partial(_out_proj_body, alpha=alpha, head_dim=head_dim, seq_tiles=seq // TM_OUT),
        grid=(n // TM_OUT,),
        in_specs=[
            pl.BlockSpec((TM_OUT, cw), lambda i: (i, 0)),
            ret_spec, ret_spec, ret_spec, ret_spec,
            pl.BlockSpec((cw, d), const),
            pl.BlockSpec((rw, d), lambda i: (cw // rw, 0)),
            pl.BlockSpec((1, d), const),
            pl.BlockSpec((TM_OUT, d), lambda i: (i, 0)),
            pl.BlockSpec((1, d), const),
            pl.BlockSpec((1, d), const),
            pl.BlockSpec((d, LANES), const),
            pl.BlockSpec((1, LANES), const),
        ],
        out_specs=[
            pl.BlockSpec((TM_OUT, d), lambda i: (i, 0)),
            pl.BlockSpec((TM_OUT, d // LANES, LANES), lambda i: (i, 0, 0)),
            pl.BlockSpec((N_EXPERTS, TM_OUT), lambda i: (0, i)),
        ],
        out_shape=[jax.ShapeDtypeStruct((n, d), F32),
                   jax.ShapeDtypeStruct((n, d // LANES, LANES), BF16),
                   jax.ShapeDtypeStruct((N_EXPERTS, n), F32)],
        scratch_shapes=[pltpu.VMEM((TM_OUT, rw), BF16),
                        pltpu.VMEM((RET_HEADS, head_dim, head_dim), F32),
                        pltpu.VMEM((RET_HEADS, CH_RET, CH_RET), F32)],
        compiler_params=_params(1),
        name="out_proj",
    )(a, q, k, v, gate, w_bf, w_bf, b_out.reshape(1, -1), x2, ln_g.reshape(1, -1),
      ln_b.reshape(1, -1), wr_pad, br_pad)


def _route_body(lt_ref, pos_ref, gate_ref, meta_ref, rank_ref):
    n_tok = lt_ref.shape[1]
    slot = float(SLOT_ROWS)
    l = lt_ref[...]
    eio = lax.broadcasted_iota(I32, (N_EXPERTS, n_tok), 0).astype(F32)
    vals, idxs = [], []
    for _ in range(TOP_K):
        m = jnp.max(l, axis=0, keepdims=True)
        idx = jnp.min(jnp.where(l == m, eio, float(N_EXPERTS)), axis=0, keepdims=True)
        vals.append(m)
        idxs.append(idx)
        l = jnp.where(eio == idx, -jnp.inf, l)
    exps = [jnp.exp(v - vals[0]) for v in vals]
    den = exps[0] + exps[1] + exps[2] + exps[3]
    for k in range(TOP_K):
        gate_ref[k:k + 1, :] = exps[k] / den

    chosen = jnp.zeros((N_EXPERTS, n_tok), F32)
    for k in range(TOP_K):
        chosen = jnp.where(eio == idxs[k], 1.0, chosen)
    tri = (lax.broadcasted_iota(I32, (PREFIX_BLK, PREFIX_BLK), 0)
           < lax.broadcasted_iota(I32, (PREFIX_BLK, PREFIX_BLK), 1)).astype(BF16)
    count = jnp.zeros((N_EXPERTS, 1), F32)
    for blk in range(n_tok // PREFIX_BLK):
        cb = chosen[:, blk * PREFIX_BLK:(blk + 1) * PREFIX_BLK]
        pre = jnp.dot(cb.astype(BF16), tri, preferred_element_type=F32)
        rank_ref[:, blk * PREFIX_BLK:(blk + 1) * PREFIX_BLK] = pre + count
        count = count + jnp.sum(cb, axis=1, keepdims=True)

    n_items = jnp.broadcast_to(jnp.floor((count + (slot - 1.0)) / slot), (N_EXPERTS, LANES))
    rio = lax.broadcasted_iota(I32, (N_EXPERTS, LANES), 0)
    inc = n_items
    shift = 1
    while shift < N_EXPERTS:
        inc = inc + jnp.where(rio >= shift, pltpu.roll(inc, shift, axis=0), 0.0)
        shift *= 2
    start = inc - n_items

    rank = rank_ref[...]
    rq = jnp.floor((rank + 0.5) / slot)
    dest = (start[:, :1] + rq) * slot + (rank - rq * slot)
    for k in range(TOP_K):
        pos_ref[k:k + 1, :] = jnp.sum(
            jnp.where(eio == idxs[k], dest, 0.0), axis=0, keepdims=True).astype(I32)

    wio = lax.broadcasted_iota(I32, (N_EXPERTS, LANES), 1).astype(F32)
    owns = (wio >= start) & (wio < start + n_items)
    rows = jnp.clip(count - (wio - start) * slot, 0.0, slot)
    item_e = jnp.sum(jnp.where(owns, rio.astype(F32), 0.0), axis=0, keepdims=True)
    item_rows = jnp.sum(jnp.where(owns, rows, 0.0), axis=0, keepdims=True)
    meta_ref[...] = jnp.zeros_like(meta_ref)
    meta_ref[0:1, :] = item_e.astype(I32)
    meta_ref[1:2, :] = item_rows.astype(I32)
    meta_ref[2:3, :] = inc[N_EXPERTS - 1:N_EXPERTS, :].astype(I32)


def _route(lt):
    n_tok = lt.shape[1]
    return pl.pallas_call(
        _route_body,
        out_shape=[jax.ShapeDtypeStruct((TOP_K, n_tok), I32),
                   jax.ShapeDtypeStruct((TOP_K, n_tok), F32),
                   jax.ShapeDtypeStruct((8, LANES), I32)],
        scratch_shapes=[pltpu.VMEM((N_EXPERTS, n_tok), F32)],
        compiler_params=pltpu.CompilerParams(vmem_limit_bytes=VMEM_LIMIT),
        name="route",
    )(lt)


def _dispatch_body(pos_ref, h_ref, xs_ref, sem):
    def row_copy(t, k):
        return pltpu.make_async_copy(h_ref.at[t], xs_ref.at[pos_ref[k, t]], sem)

    def issue(t, carry):
        for k in range(TOP_K):
            row_copy(t, k).start(priority=k % 2)
        return carry

    lax.fori_loop(0, TB_DISP, issue, 0, unroll=4)
    for _ in range(TOP_K):
        pltpu.make_async_copy(h_ref, xs_ref.at[pl.ds(0, TB_DISP)], sem).wait()


def _dispatch(pos, hp, n_rows):
    n_tok, subl, lanes = hp.shape
    return pl.pallas_call(
        _dispatch_body,
        grid=(n_tok // TB_DISP,),
        in_specs=[
            pl.BlockSpec((TOP_K, TB_DISP), lambda i: (0, i), memory_space=pltpu.SMEM),
            pl.BlockSpec((TB_DISP, subl, lanes), lambda i: (i, 0, 0)),
        ],
        out_specs=pl.BlockSpec(memory_space=pl.ANY),
        out_shape=jax.ShapeDtypeStruct((n_rows, subl, lanes), BF16),
        scratch_shapes=[pltpu.SemaphoreType.DMA],
        compiler_params=_params(1),
        name="dispatch",
    )(pos, hp)


def _expert_body(ie_ref, nr_ref, na_ref, xs_ref, wg_ref, wl_ref, bg_ref, bl_ref, wd_ref, bd_ref,
                 ys_ref, xb_ref, acc_ref, *, n_f):
    w = pl.program_id(0)
    f = pl.program_id(1)

    @pl.when(w < na_ref[0])
    def _():
        n_rows = nr_ref[w]

        @pl.when(f == 0)
        def _():
            xb_ref[...] = _from_row_tiles(xs_ref[...])
            last0 = ((n_rows + (ROW_BLK - 1)) // ROW_BLK - 1) * ROW_BLK
            last = pl.ds(pl.multiple_of(last0, ROW_BLK), ROW_BLK)
            row = lax.broadcasted_iota(I32, (ROW_BLK, xb_ref.shape[1]), 0) + last0
            xb_ref[last, :] = jnp.where(row < n_rows, xb_ref[last, :], jnp.zeros((), BF16))

        bg = bg_ref[0]
        bl = bl_ref[0]

        def mlp_rows(n):
            xblk = xb_ref[0:n, :]
            hg = jnp.dot(xblk, wg_ref[0].astype(BF16), preferred_element_type=F32) + bg
            hl = jnp.dot(xblk, wl_ref[0].astype(BF16), preferred_element_type=F32) + bl
            x_glu = jnp.minimum(hg, SWIGLU_LIMIT)
            x_lin = jnp.clip(hl, -SWIGLU_LIMIT, SWIGLU_LIMIT)
            act = x_glu * _sigmoid(SWIGLU_ALPHA * x_glu) * (x_lin + 1.0)
            start = jnp.where(f == 0, bd_ref[0], acc_ref[0:n, :])
            acc_ref[0:n, :] = start + jnp.dot(
                act.astype(BF16), wd_ref[0].astype(BF16), preferred_element_type=F32)

            @pl.when(f == n_f - 1)
            def _():
                ys_ref[0:n] = _to_row_tiles(acc_ref[0:n, :])

        n_sub = (n_rows + (ROW_BLK - 1)) // ROW_BLK
        for m in range(1, SLOT_ROWS // ROW_BLK + 1):
            @pl.when(n_sub == m)
            def _(m=m):
                mlp_rows(m * ROW_BLK)


def _experts(item_e, item_rows, n_active, xs, w_gate_up, b_gate_up, w_down, b_down, *, w_max):
    n_e, d, f2 = w_gate_up.shape
    d_exp = f2 // 2
    n_f = d_exp // TF_EXP
    slot_block = (SLOT_ROWS,) + xs.shape[1:]

    def item(w, na):
        return jnp.minimum(w, na[0] - 1)

    def fcol(w, f, na):
        return jnp.where(w < na[0], f, n_f - 1)

    def slot_idx(w, f, ie, nr, na):
        return (item(w, na), 0, 0)

    def slot_in_idx(w, f, ie, nr, na):
        return (item(w + (f >= n_f // 2).astype(I32), na), 0, 0)

    grid_spec = pltpu.PrefetchScalarGridSpec(
        num_scalar_prefetch=3,
        grid=(w_max, n_f),
        in_specs=[
            pl.BlockSpec(slot_block, slot_in_idx),
            pl.BlockSpec((1, d, TF_EXP), lambda w, f, ie, nr, na: (ie[item(w, na)], 0, fcol(w, f, na))),
            pl.BlockSpec((1, d, TF_EXP),
                         lambda w, f, ie, nr, na: (ie[item(w, na)], 0, n_f + fcol(w, f, na))),
            pl.BlockSpec((1, 1, TF_EXP), lambda w, f, ie, nr, na: (ie[item(w, na)], 0, fcol(w, f, na))),
            pl.BlockSpec((1, 1, TF_EXP),
                         lambda w, f, ie, nr, na: (ie[item(w, na)], 0, n_f + fcol(w, f, na))),
            pl.BlockSpec((1, TF_EXP, d), lambda w, f, ie, nr, na: (ie[item(w, na)], fcol(w, f, na), 0)),
            pl.BlockSpec((1, 1, d), lambda w, f, ie, nr, na: (ie[item(w, na)], 0, 0)),
        ],
        out_specs=pl.BlockSpec(slot_block, slot_idx),
        scratch_shapes=[
            pltpu.VMEM((SLOT_ROWS, d), BF16),
            pltpu.VMEM((SLOT_ROWS, d), F32),
        ],
    )
    bgu = b_gate_up.reshape(n_e, 1, f2)
    return pl.pallas_call(
        functools.partial(_expert_body, n_f=n_f),
        grid_spec=grid_spec,
        out_shape=jax.ShapeDtypeStruct(xs.shape, BF16),
        compiler_params=_params(2),
        name="experts",
    )(item_e, item_rows, n_active, xs, w_gate_up, w_gate_up, bgu, bgu, w_down,
      b_down.reshape(n_e, 1, d))


def _combine_body(pos_ref, pos_next_ref, gt_ref, h_ref, g_ref, b_ref, ys_ref, o_ref, buf_ref, sem,
                  *, alpha):
    i = pl.program_id(0)
    slot = i % 2

    def gather(p_ref, s):
        def issue(t, carry):
            for k in range(TOP_K):
                pltpu.make_async_copy(
                    ys_ref.at[p_ref[k, t]], buf_ref.at[s, k, t], sem.at[s]).start(priority=k % 2)
            return carry

        lax.fori_loop(0, TB_COMB, issue, 0, unroll=4)

    @pl.when(i == 0)
    def _():
        gather(pos_ref, 0)

    @pl.when(i + 1 < pl.num_programs(0))
    def _():
        gather(pos_next_ref, 1 - slot)

    for k in range(TOP_K):
        pltpu.make_async_copy(
            ys_ref.at[pl.ds(0, TB_COMB)], buf_ref.at[slot, k], sem.at[slot]).wait()

    ffn = jnp.zeros(h_ref.shape, F32)
    for k in range(TOP_K):
        ffn = ffn + gt_ref[:, k:k + 1] * _from_row_tiles(buf_ref[slot, k]).astype(F32)
    y = alpha * h_ref[...] + ffn
    mu = jnp.mean(y, axis=-1, keepdims=True)
    dev = y - mu
    var = jnp.mean(dev * dev, axis=-1, keepdims=True)
    o_ref[...] = dev * lax.rsqrt(var + LN_EPS) * g_ref[...] + b_ref[...]


def _combine(pos, gates_t, h1, ln_g, ln_b, ys, *, alpha):
    n_tok, d = h1.shape
    const = lambda i: (0, 0)
    n_tiles = n_tok // TB_COMB
    return pl.pallas_call(
        functools.partial(_combine_body, alpha=alpha),
        grid=(n_tiles,),
        in_specs=[
            pl.BlockSpec((TOP_K, TB_COMB), lambda i: (0, i), memory_space=pltpu.SMEM),
            pl.BlockSpec((TOP_K, TB_COMB), lambda i: (0, jnp.minimum(i + 1, n_tiles - 1)),
                         memory_space=pltpu.SMEM),
            pl.BlockSpec((TB_COMB, TOP_K), lambda i: (i, 0)),
            pl.BlockSpec((TB_COMB, d), lambda i: (i, 0)),
            pl.BlockSpec((1, d), const),
            pl.BlockSpec((1, d), const),
            pl.BlockSpec(memory_space=pl.ANY),
        ],
        out_specs=pl.BlockSpec((TB_COMB, d), lambda i: (i, 0)),
        out_shape=jax.ShapeDtypeStruct((n_tok, d), F32),
        scratch_shapes=[pltpu.VMEM((2, TOP_K, TB_COMB) + ys.shape[1:], BF16),
                        pltpu.SemaphoreType.DMA((2,))],
        compiler_params=_params(1),
        name="combine",
    )(pos, pos, gates_t, h1, ln_g.reshape(1, -1), ln_b.reshape(1, -1), ys)


def _rope_tables(seq, half):
    inv_freq = ROPE_BASE ** (-jnp.arange(half, dtype=F32) / half)
    ang = jnp.arange(seq, dtype=F32)[:, None] * inv_freq[None, :]
    return jnp.cos(ang), jnp.sin(ang)


def _layer(h, p, *, batch, seq, alpha, cos, sin):
    n_tok, d = h.shape
    conv_w = p["conv_w"].shape[1]
    ret_w = d - conv_w
    a, q, k, v, g = _in_proj(h, p["w_in"], p["b_in"], cos, sin, p["conv_w"], p["conv_b"],
                             p["conv_ln_g"], p["conv_ln_b"], seq=seq, conv_w=conv_w, ret_w=ret_w)
    h1, h1_packed, logits_t = _out_proj(a, q, k, v, g, p["w_out"], p["b_out"], h, p["ln1_g"],
                                        p["ln1_b"], p["w_router"], p["b_router"], alpha=alpha, seq=seq)
    pos, gates, meta = _route(logits_t)
    w_max = N_EXPERTS + (n_tok * TOP_K) // SLOT_ROWS
    xs = _dispatch(pos, h1_packed, w_max * SLOT_ROWS)
    ys = _experts(meta[0, :w_max], meta[1, :w_max], meta[2, :1], xs,
                  p["w_gate_up"], p["b_gate_up"], p["w_down"], p["b_down"], w_max=w_max)
    return _combine(pos, gates.T, h1, p["ln2_g"], p["ln2_b"], ys, alpha=alpha)


def kernel(x, w_in, b_in, conv_w, conv_b, conv_ln_g, conv_ln_b, w_out, b_out, ln1_g, ln1_b,
           w_router, b_router, w_gate_up, b_gate_up, w_down, b_down, ln2_g, ln2_b):
    batch, seq, d = x.shape
    depth = w_in.shape[0]
    alpha = float((2 * depth) ** 0.25)
    head_dim = (d - conv_w.shape[2]) // RET_HEADS
    cos, sin = _rope_tables(seq, head_dim // 2)
    stacked = dict(w_in=w_in, b_in=b_in, conv_w=conv_w, conv_b=conv_b, conv_ln_g=conv_ln_g,
                   conv_ln_b=conv_ln_b, w_out=w_out, b_out=b_out, ln1_g=ln1_g, ln1_b=ln1_b,
                   w_router=w_router, b_router=b_router, w_gate_up=w_gate_up, b_gate_up=b_gate_up,
                   w_down=w_down, b_down=b_down, ln2_g=ln2_g, ln2_b=ln2_b)
    h = x.reshape(batch * seq, d)
    for layer in range(depth):
        p = {name: val[layer] for name, val in stacked.items()}
        h = _layer(h, p, batch=batch, seq=seq, alpha=alpha, cos=cos, sin=sin)
    return h.reshape(batch, seq, d)
```

```python
import functools
import math

import jax
import jax.numpy as jnp
from jax import lax
from jax.experimental import pallas as pl
from jax.experimental.pallas import tpu as pltpu

F32 = jnp.float32
BF16 = jnp.bfloat16
I32 = jnp.int32

RET_HEADS = 4
CONV_KERNEL = 31
ROPE_BASE = 10000.0
N_EXPERTS = 32
TOP_K = 4
SWIGLU_ALPHA = 1.702
SWIGLU_LIMIT = 7.0
LN_EPS = 1e-5
LOG_GAMMA = tuple(math.log(1.0 - 2.0 ** (-5.0 - h)) for h in range(RET_HEADS))

LANES = 128
SUBLANES = 8
VMEM_LIMIT = 58 * 1024 * 1024

TM_IN = 1024
TN_IN = 512
CONV_PIECE = 128
HALO = 32
RC_CONV = 128
CH_RET = 256
TM_OUT = 512
PREFIX_BLK = 256
SLOT_ROWS = 1280
ROW_BLK = 128
TF_EXP = 256
TB_DISP = 1024
TB_COMB = 128


def _params(n_axes):
    return pltpu.CompilerParams(
        dimension_semantics=("arbitrary",) * n_axes, vmem_limit_bytes=VMEM_LIMIT)


def _sigmoid(x):
    return 1.0 / (1.0 + jnp.exp(-x))


def _to_row_tiles(v):
    return v.astype(BF16).reshape(v.shape[0], v.shape[1] // LANES, LANES)


def _from_row_tiles(t):
    return t.reshape(t.shape[0], t.shape[1] * t.shape[2])


def _in_proj_body(x_ref, wa_ref, wb_ref, ba_ref, bb_ref, cos_ref, sin_ref, cw_ref, cb_ref, cg_ref, cbt_ref,
                  a_ref, q_ref, k_ref, v_ref, g_ref, xb_ref, u_ref, halo_ref, ext_ref, sh_ref,
                  *, n_a, n_r, head_dim, seq_tiles):
    i = pl.program_id(0)
    j = pl.program_id(1)
    width = u_ref.shape[1]

    @pl.when(j == 0)
    def _():
        xb_ref[...] = x_ref[...].astype(BF16)
        halo_ref[...] = u_ref[TM_IN - HALO:, :]

    pieces = [slice(c, c + head_dim) for c in range(0, TN_IN, head_dim)]

    def project(w_ref, b_ref, cols):
        return jnp.dot(xb_ref[...], w_ref[:, cols], preferred_element_type=F32) + b_ref[:, cols]

    for ja in range(n_a):
        @pl.when(j == ja)
        def _(ja=ja):
            for cols in pieces:
                val = project(wa_ref, ba_ref, cols)
                gate = project(wb_ref, bb_ref, cols)
                u_ref[:, ja * TN_IN + cols.start:ja * TN_IN + cols.stop] = (
                    val * _sigmoid(gate)).astype(BF16)

    def conv_piece():
        p = j - n_a
        r0 = pl.multiple_of(p * CONV_PIECE, CONV_PIECE)
        above = u_ref[pl.ds(pl.multiple_of(jnp.maximum(r0 - HALO, 0), HALO), HALO), :].astype(F32)
        halo = jnp.where(i % seq_tiles == 0, 0.0, halo_ref[...].astype(F32))
        ext_ref[0:HALO, :] = jnp.where(p == 0, halo, above)
        ext_ref[HALO:, :] = u_ref[pl.ds(r0, CONV_PIECE), :].astype(F32)
        first = HALO - (CONV_KERNEL - 1)
        ext_rows = ext_ref.shape[0]
        for s in range(SUBLANES):
            sh_ref[s, 0:ext_rows - SUBLANES, :] = ext_ref[s:s + ext_rows - SUBLANES, :]
        sh_ref[0, ext_rows - SUBLANES:, :] = ext_ref[ext_rows - SUBLANES:, :]
        for c0 in range(0, CONV_PIECE, RC_CONV):
            acc = jnp.broadcast_to(cb_ref[...], (RC_CONV, width))
            for tap in range(CONV_KERNEL):
                off = c0 + first + tap
                lo = off - off % SUBLANES
                acc = acc + cw_ref[tap:tap + 1, :] * sh_ref[off % SUBLANES, lo:lo + RC_CONV, :]
            mu = jnp.mean(acc, axis=-1, keepdims=True)
            dev = acc - mu
            var = jnp.mean(dev * dev, axis=-1, keepdims=True)
            y = dev * lax.rsqrt(var + LN_EPS) * cg_ref[...] + cbt_ref[...]
            a_ref[pl.ds(r0 + c0, RC_CONV), :] = (y * _sigmoid(y)).astype(BF16)

    def rotary(o_ref, scale):
        half = head_dim // 2
        for cols in pieces:
            acc = project(wa_ref, ba_ref, cols)
            t1 = acc[:, :half]
            t2 = acc[:, half:]
            cos = cos_ref[...]
            sin = sin_ref[...]
            o_ref[:, cols.start:cols.start + half] = ((t1 * cos - t2 * sin) * scale).astype(BF16)
            o_ref[:, cols.start + half:cols.stop] = ((t1 * sin + t2 * cos) * scale).astype(BF16)

    @pl.when((j >= n_a) & (j < n_a + n_r))
    def _():
        conv_piece()
        rotary(q_ref, 1.0)

    @pl.when((j >= n_a + n_r) & (j < n_a + 2 * n_r))
    def _():
        conv_piece()
        rotary(k_ref, head_dim ** -0.5)

    @pl.when((j >= n_a + 2 * n_r) & (j < n_a + 3 * n_r))
    def _():
        conv_piece()
        for cols in pieces:
            v_ref[:, cols] = project(wa_ref, ba_ref, cols).astype(BF16)

    @pl.when(j >= n_a + 3 * n_r)
    def _():
        conv_piece()
        for cols in pieces:
            acc = project(wa_ref, ba_ref, cols)
            g_ref[:, cols] = (acc * _sigmoid(acc)).astype(BF16)


def _in_proj(x2, w_in, b_in, cos, sin, conv_taps, conv_b, conv_g, conv_bt, *, seq, conv_w, ret_w):
    n, d = x2.shape
    head_dim = ret_w // RET_HEADS
    n_a = conv_w // TN_IN
    n_r = ret_w // TN_IN
    n_j = n_a + 4 * n_r
    assert CONV_PIECE * 4 * n_r == TM_IN and seq % TM_IN == 0
    seq_tiles = seq // TM_IN
    b2 = b_in.reshape(1, -1)
    w_bf = w_in.astype(BF16)

    def wa_idx(i, j):
        return (0, jnp.where(j < n_a, j, j + n_a))

    def wb_idx(i, j):
        return (0, jnp.where(j < n_a, n_a + j, 2 * n_a - 1))

    def seg(lo, cnt):
        return lambda i, j: (i, jnp.clip(j - lo, 0, cnt - 1))

    const = lambda i, j: (0, 0)
    out_a = jax.ShapeDtypeStruct((n, conv_w), BF16)
    out_r = jax.ShapeDtypeStruct((n, ret_w), BF16)
    return pl.pallas_call(
        functools.partial(_in_proj_body, n_a=n_a, n_r=n_r, head_dim=head_dim, seq_tiles=seq_tiles),
        grid=(n // TM_IN, n_j),
        in_specs=[
            pl.BlockSpec((TM_IN, d), lambda i, j: (i, 0)),
            pl.BlockSpec((d, TN_IN), wa_idx),
            pl.BlockSpec((d, TN_IN), wb_idx),
            pl.BlockSpec((1, TN_IN), wa_idx),
            pl.BlockSpec((1, TN_IN), wb_idx),
            pl.BlockSpec((TM_IN, head_dim // 2), lambda i, j: (i % seq_tiles, 0)),
            pl.BlockSpec((TM_IN, head_dim // 2), lambda i, j: (i % seq_tiles, 0)),
            pl.BlockSpec((CONV_KERNEL, conv_w), const),
            pl.BlockSpec((1, conv_w), const),
            pl.BlockSpec((1, conv_w), const),
            pl.BlockSpec((1, conv_w), const),
        ],
        out_specs=[
            pl.BlockSpec((TM_IN, conv_w), lambda i, j: (i, 0)),
            pl.BlockSpec((TM_IN, TN_IN), seg(n_a, n_r)),
            pl.BlockSpec((TM_IN, TN_IN), seg(n_a + n_r, n_r)),
            pl.BlockSpec((TM_IN, TN_IN), seg(n_a + 2 * n_r, n_r)),
            pl.BlockSpec((TM_IN, TN_IN), seg(n_a + 3 * n_r, n_r)),
        ],
        out_shape=[out_a, out_r, out_r, out_r, out_r],
        scratch_shapes=[pltpu.VMEM((TM_IN, d), BF16),
                        pltpu.VMEM((TM_IN, conv_w), BF16),
                        pltpu.VMEM((HALO, conv_w), BF16),
                        pltpu.VMEM((HALO + CONV_PIECE, conv_w), F32),
                        pltpu.VMEM((SUBLANES, HALO + CONV_PIECE, conv_w), F32)],
        compiler_params=_params(2),
        name="in_proj",
    )(x2, w_bf, w_bf, b2, b2, cos, sin, conv_taps, conv_b.reshape(1, -1), conv_g.reshape(1, -1),
      conv_bt.reshape(1, -1))


def _retention_mask_init(mask_ref):
    rel = (lax.broadcasted_iota(I32, (CH_RET, CH_RET), 0)
           - lax.broadcasted_iota(I32, (CH_RET, CH_RET), 1)).astype(F32)
    for h in range(RET_HEADS):
        mask_ref[h] = jnp.where(rel >= 0, jnp.exp(LOG_GAMMA[h] * jnp.maximum(rel, 0.0)), 0.0)


def _retention_chunk(q_ref, k_ref, v_ref, g_ref, r_ref, state_ref, mask_ref, rows, head_dim):
    idx = lax.broadcasted_iota(I32, (CH_RET, 1), 0).astype(F32)
    nt = (((1,), (1,)), ((), ()))
    tn = (((0,), (0,)), ((), ()))
    for h in range(RET_HEADS):
        sl = slice(h * head_dim, (h + 1) * head_dim)
        qh = q_ref[rows, sl]
        kh = k_ref[rows, sl]
        vh = v_ref[rows, sl]
        scores = lax.dot_general(qh, kh, nt, preferred_element_type=F32) * mask_ref[h]
        inner = jnp.dot(scores.astype(BF16), vh, preferred_element_type=F32)
        st = state_ref[h]
        q_decay = jnp.exp(LOG_GAMMA[h] * (idx + 1.0))
        cross = jnp.dot(qh, st.astype(BF16), preferred_element_type=F32) * q_decay
        k_decay = jnp.exp(LOG_GAMMA[h] * (CH_RET - 1.0 - idx))
        k_dec = (kh.astype(F32) * k_decay).astype(BF16)
        kv = lax.dot_general(k_dec, vh, tn, preferred_element_type=F32)
        state_ref[h] = math.exp(LOG_GAMMA[h] * CH_RET) * st + kv
        o = inner + cross
        mu = jnp.mean(o, axis=-1, keepdims=True)
        dev = o - mu
        var = jnp.mean(dev * dev, axis=-1, keepdims=True)
        r_ref[rows, sl] = (g_ref[rows, sl].astype(F32) * (dev * lax.rsqrt(var + LN_EPS))).astype(BF16)


def _out_proj_body(a_ref, q_ref, k_ref, v_ref, gate_ref, wt_ref, wb_ref, bo_ref, x_ref, g_ref, b_ref,
                   wr_ref, br_ref, h_ref, hp_ref, lt_ref, r_ref, state_ref, mask_ref,
                   *, alpha, head_dim, seq_tiles):
    i = pl.program_id(0)

    @pl.when(i == 0)
    def _():
        _retention_mask_init(mask_ref)

    @pl.when(i % seq_tiles == 0)
    def _():
        state_ref[...] = jnp.zeros_like(state_ref)

    for c0 in range(0, TM_OUT, CH_RET):
        _retention_chunk(q_ref, k_ref, v_ref, gate_ref, r_ref, state_ref, mask_ref,
                         slice(c0, c0 + CH_RET), head_dim)
    mix = (jnp.dot(a_ref[...], wt_ref[...], preferred_element_type=F32)
           + jnp.dot(r_ref[...], wb_ref[...], preferred_element_type=F32) + bo_ref[...])
    y = alpha * x_ref[...] + mix
    mu = jnp.mean(y, axis=-1, keepdims=True)
    dev = y - mu
    var = jnp.mean(dev * dev, axis=-1, keepdims=True)
    h1 = dev * lax.rsqrt(var + LN_EPS) * g_ref[...] + b_ref[...]
    h_ref[...] = h1
    hp_ref[...] = _to_row_tiles(h1)
    h_hi = h1.astype(BF16)
    h_lo = (h1 - h_hi.astype(F32)).astype(BF16)
    wr = wr_ref[...]
    w_hi = wr.astype(BF16)
    w_lo = (wr - w_hi.astype(F32)).astype(BF16)
    hi_both = jnp.dot(h_hi, jnp.concatenate([w_hi, w_lo], axis=1), preferred_element_type=F32)
    logits = (hi_both[:, :LANES] + hi_both[:, LANES:]
              + jnp.dot(h_lo, w_hi, preferred_element_type=F32) + br_ref[...])
    lt_ref[...] = logits.T[:N_EXPERTS, :]


def _out_proj(a, q, k, v, gate, w_out, b_out, x2, ln_g, ln_b, w_router, b_router, *, alpha, seq):
    n, d = x2.shape
    cw = a.shape[1]
    rw = q.shape[1]
    head_dim = rw // RET_HEADS
    assert seq % TM_OUT == 0 and TM_OUT % CH_RET == 0
    w_bf = w_out.astype(BF16)
    wr_pad = jnp.pad(w_router, ((0, 0), (0, LANES - N_EXPERTS)))
    br_pad = jnp.pad(b_router, (0, LANES - N_EXPERTS)).reshape(1, LANES)
    const = lambda i: (0, 0)
    ret_spec = pl.BlockSpec((TM_OUT, rw), lambda i: (i, 0))
    return pl.pallas_call(
        functools.partial(_out_proj_body, alpha=alpha, head_dim=head_dim, seq_tiles=seq // TM_OUT),
        grid=(n // TM_OUT,),
        in_specs=[
            pl.BlockSpec((TM_OUT, cw), lambda i: (i, 0)),
            ret_spec, ret_spec, ret_spec, ret_spec,
            pl.BlockSpec((cw, d), const),
            pl.BlockSpec((rw, d), lambda i: (cw // rw, 0)),
            pl.BlockSpec((1, d), const),
            pl.BlockSpec((TM_OUT, d), lambda i: (i, 0)),
            pl.BlockSpec((1, d), const),
            pl.BlockSpec((1, d), const),
            pl.BlockSpec((d, LANES), const),
            pl.BlockSpec((1, LANES), const),
        ],
        out_specs=[
            pl.BlockSpec((TM_OUT, d), lambda i: (i, 0)),
            pl.BlockSpec((TM_OUT, d // LANES, LANES), lambda i: (i, 0, 0)),
            pl.BlockSpec((N_EXPERTS, TM_OUT), lambda i: (0, i)),
        ],
        out_shape=[jax.ShapeDtypeStruct((n, d), F32),
                   jax.ShapeDtypeStruct((n, d // LANES, LANES), BF16),
                   jax.ShapeDtypeStruct((N_EXPERTS, n), F32)],
        scratch_shapes=[pltpu.VMEM((TM_OUT, rw), BF16),
                        pltpu.VMEM((RET_HEADS, head_dim, head_dim), F32),
                        pltpu.VMEM((RET_HEADS, CH_RET, CH_RET), F32)],
        compiler_params=_params(1),
        name="out_proj",
    )(a, q, k, v, gate, w_bf, w_bf, b_out.reshape(1, -1), x2, ln_g.reshape(1, -1),
      ln_b.reshape(1, -1), wr_pad, br_pad)


def _route_body(lt_ref, pos_ref, gate_ref, meta_ref, rank_ref):
    n_tok = lt_ref.shape[1]
    slot = float(SLOT_ROWS)
    l = lt_ref[...]
    eio = lax.broadcasted_iota(I32, (N_EXPERTS, n_tok), 0).astype(F32)
    vals, idxs = [], []
    for _ in range(TOP_K):
        m = jnp.max(l, axis=0, keepdims=True)
        idx = jnp.min(jnp.where(l == m, eio, float(N_EXPERTS)), axis=0, keepdims=True)
        vals.append(m)
        idxs.append(idx)
        l = jnp.where(eio == idx, -jnp.inf, l)
    exps = [jnp.exp(v - vals[0]) for v in vals]
    den = exps[0] + exps[1] + exps[2] + exps[3]
    for k in range(TOP_K):
        gate_ref[k:k + 1, :] = exps[k] / den

    chosen = jnp.zeros((N_EXPERTS, n_tok), F32)
    for k in range(TOP_K):
        chosen = jnp.where(eio == idxs[k], 1.0, chosen)
    tri = (lax.broadcasted_iota(I32, (PREFIX_BLK, PREFIX_BLK), 0)
           < lax.broadcasted_iota(I32, (PREFIX_BLK, PREFIX_BLK), 1)).astype(BF16)
    count = jnp.zeros((N_EXPERTS, 1), F32)
    for blk in range(n_tok // PREFIX_BLK):
        cb = chosen[:, blk * PREFIX_BLK:(blk + 1) * PREFIX_BLK]
        pre = jnp.dot(cb.astype(BF16), tri, preferred_element_type=F32)
        rank_ref[:, blk * PREFIX_BLK:(blk + 1) * PREFIX_BLK] = pre + count
        count = count + jnp.sum(cb, axis=1, keepdims=True)

    n_items = jnp.broadcast_to(jnp.floor((count + (slot - 1.0)) / slot), (N_EXPERTS, LANES))
    rio = lax.broadcasted_iota(I32, (N_EXPERTS, LANES), 0)
    inc = n_items
    shift = 1
    while shift < N_EXPERTS:
        inc = inc + jnp.where(rio >= shift, pltpu.roll(inc, shift, axis=0), 0.0)
        shift *= 2
    start = inc - n_items

    rank = rank_ref[...]
    rq = jnp.floor((rank + 0.5) / slot)
    dest = (start[:, :1] + rq) * slot + (rank - rq * slot)
    for k in range(TOP_K):
        pos_ref[k:k + 1, :] = jnp.sum(
            jnp.where(eio == idxs[k], dest, 0.0), axis=0, keepdims=True).astype(I32)

    wio = lax.broadcasted_iota(I32, (N_EXPERTS, LANES), 1).astype(F32)
    owns = (wio >= start) & (wio < start + n_items)
    rows = jnp.clip(count - (wio - start) * slot, 0.0, slot)
    item_e = jnp.sum(jnp.where(owns, rio.astype(F32), 0.0), axis=0, keepdims=True)
    item_rows = jnp.sum(jnp.where(owns, rows, 0.0), axis=0, keepdims=True)
    meta_ref[...] = jnp.zeros_like(meta_ref)
    meta_ref[0:1, :] = item_e.astype(I32)
    meta_ref[1:2, :] = item_rows.astype(I32)
    meta_ref[2:3, :] = inc[N_EXPERTS - 1:N_EXPERTS, :].astype(I32)


def _route(lt):
    n_tok = lt.shape[1]
    return pl.pallas_call(
        _route_body,
        out_shape=[jax.ShapeDtypeStruct((TOP_K, n_tok), I32),
                   jax.ShapeDtypeStruct((TOP_K, n_tok), F32),
                   jax.ShapeDtypeStruct((8, LANES), I32)],
        scratch_shapes=[pltpu.VMEM((N_EXPERTS, n_tok), F32)],
        compiler_params=pltpu.CompilerParams(vmem_limit_bytes=VMEM_LIMIT),
        name="route",
    )(lt)


def _dispatch_body(pos_ref, h_ref, xs_ref, sem):
    def row_copy(t, k):
        return pltpu.make_async_copy(h_ref.at[t], xs_ref.at[pos_ref[k, t]], sem)

    def issue(t, carry):
        for k in range(TOP_K):
            row_copy(t, k).start(priority=k % 2)
        return carry

    lax.fori_loop(0, TB_DISP, issue, 0, unroll=16)
    for _ in range(TOP_K):
        pltpu.make_async_copy(h_ref, xs_ref.at[pl.ds(0, TB_DISP)], sem).wait()


def _dispatch(pos, hp, n_rows):
    n_tok, subl, lanes = hp.shape
    return pl.pallas_call(
        _dispatch_body,
        grid=(n_tok // TB_DISP,),
        in_specs=[
            pl.BlockSpec((TOP_K, TB_DISP), lambda i: (0, i), memory_space=pltpu.SMEM),
            pl.BlockSpec((TB_DISP, subl, lanes), lambda i: (i, 0, 0)),
        ],
        out_specs=pl.BlockSpec(memory_space=pl.ANY),
        out_shape=jax.ShapeDtypeStruct((n_rows, subl, lanes), BF16),
        scratch_shapes=[pltpu.SemaphoreType.DMA],
        compiler_params=_params(1),
        name="dispatch",
    )(pos, hp)


def _expert_body(ie_ref, nr_ref, na_ref, xs_ref, wg_ref, wl_ref, bg_ref, bl_ref, wd_ref, bd_ref,
                 ys_ref, xb_ref, acc_ref, *, n_f):
    w = pl.program_id(0)
    f = pl.program_id(1)

    @pl.when(w < na_ref[0])
    def _():
        n_rows = nr_ref[w]

        @pl.when(f == 0)
        def _():
            xb_ref[...] = _from_row_tiles(xs_ref[...])
            last0 = ((n_rows + (ROW_BLK - 1)) // ROW_BLK - 1) * ROW_BLK
            last = pl.ds(pl.multiple_of(last0, ROW_BLK), ROW_BLK)
            row = lax.broadcasted_iota(I32, (ROW_BLK, xb_ref.shape[1]), 0) + last0
            xb_ref[last, :] = jnp.where(row < n_rows, xb_ref[last, :], jnp.zeros((), BF16))

        bg = bg_ref[0]
        bl = bl_ref[0]

        def mlp_rows(n):
            xblk = xb_ref[0:n, :]
            hg = jnp.dot(xblk, wg_ref[0].astype(BF16), preferred_element_type=F32) + bg
            hl = jnp.dot(xblk, wl_ref[0].astype(BF16), preferred_element_type=F32) + bl
            x_glu = jnp.minimum(hg, SWIGLU_LIMIT)
            x_lin = jnp.clip(hl, -SWIGLU_LIMIT, SWIGLU_LIMIT)
            act = x_glu * _sigmoid(SWIGLU_ALPHA * x_glu) * (x_lin + 1.0)
            start = jnp.where(f == 0, bd_ref[0], acc_ref[0:n, :])
            acc_ref[0:n, :] = start + jnp.dot(
                act.astype(BF16), wd_ref[0].astype(BF16), preferred_element_type=F32)

            @pl.when(f == n_f - 1)
            def _():
                ys_ref[0:n] = _to_row_tiles(acc_ref[0:n, :])

        n_sub = (n_rows + (ROW_BLK - 1)) // ROW_BLK
        for m in range(1, SLOT_ROWS // ROW_BLK + 1):
            @pl.when(n_sub == m)
            def _(m=m):
                mlp_rows(m * ROW_BLK)


def _experts(item_e, item_rows, n_active, xs, w_gate_up, b_gate_up, w_down, b_down, *, w_max):
    n_e, d, f2 = w_gate_up.shape
    d_exp = f2 // 2
    n_f = d_exp // TF_EXP
    slot_block = (SLOT_ROWS,) + xs.shape[1:]

    def item(w, na):
        return jnp.minimum(w, na[0] - 1)

    def fcol(w, f, na):
        return jnp.where(w < na[0], f, n_f - 1)

    def slot_idx(w, f, ie, nr, na):
        return (item(w, na), 0, 0)

    def slot_in_idx(w, f, ie, nr, na):
        return (item(w + (f >= n_f // 2).astype(I32), na), 0, 0)

    grid_spec = pltpu.PrefetchScalarGridSpec(
        num_scalar_prefetch=3,
        grid=(w_max, n_f),
        in_specs=[
            pl.BlockSpec(slot_block, slot_in_idx),
            pl.BlockSpec((1, d, TF_EXP), lambda w, f, ie, nr, na: (ie[item(w, na)], 0, fcol(w, f, na))),
            pl.BlockSpec((1, d, TF_EXP),
                         lambda w, f, ie, nr, na: (ie[item(w, na)], 0, n_f + fcol(w, f, na))),
            pl.BlockSpec((1, 1, TF_EXP), lambda w, f, ie, nr, na: (ie[item(w, na)], 0, fcol(w, f, na))),
            pl.BlockSpec((1, 1, TF_EXP),
                         lambda w, f, ie, nr, na: (ie[item(w, na)], 0, n_f + fcol(w, f, na))),
            pl.BlockSpec((1, TF_EXP, d), lambda w, f, ie, nr, na: (ie[item(w, na)], fcol(w, f, na), 0)),
            pl.BlockSpec((1, 1, d), lambda w, f, ie, nr, na: (ie[item(w, na)], 0, 0)),
        ],
        out_specs=pl.BlockSpec(slot_block, slot_idx),
        scratch_shapes=[
            pltpu.VMEM((SLOT_ROWS, d), BF16),
            pltpu.VMEM((SLOT_ROWS, d), F32),
        ],
    )
    bgu = b_gate_up.reshape(n_e, 1, f2)
    return pl.pallas_call(
        functools.partial(_expert_body, n_f=n_f),
        grid_spec=grid_spec,
        out_shape=jax.ShapeDtypeStruct(xs.shape, BF16),
        compiler_params=_params(2),
        name="experts",
    )(item_e, item_rows, n_active, xs, w_gate_up, w_gate_up, bgu, bgu, w_down,
      b_down.reshape(n_e, 1, d))


def _combine_body(pos_ref, pos_next_ref, gt_ref, h_ref, g_ref, b_ref, ys_ref, o_ref, buf_ref, sem,
                  *, alpha):
    i = pl.program_id(0)
    slot = i % 2

    def gather(p_ref, s):
        def issue(t, carry):
            for k in range(TOP_K):
                pltpu.make_async_copy(
                    ys_ref.at[p_ref[k, t]], buf_ref.at[s, k, t], sem.at[s]).start(priority=k % 2)
            return carry

        lax.fori_loop(0, TB_COMB, issue, 0, unroll=16)

    @pl.when(i == 0)
    def _():
        gather(pos_ref, 0)

    @pl.when(i + 1 < pl.num_programs(0))
    def _():
        gather(pos_next_ref, 1 - slot)

    for k in range(TOP_K):
        pltpu.make_async_copy(
            ys_ref.at[pl.ds(0, TB_COMB)], buf_ref.at[slot, k], sem.at[slot]).wait()

    ffn = jnp.zeros(h_ref.shape, F32)
    for k in range(TOP_K):
        ffn = ffn + gt_ref[:, k:k + 1] * _from_row_tiles(buf_ref[slot, k]).astype(F32)
    y = alpha * h_ref[...] + ffn
    mu = jnp.mean(y, axis=-1, keepdims=True)
    dev = y - mu
    var = jnp.mean(dev * dev, axis=-1, keepdims=True)
    o_ref[...] = dev * lax.rsqrt(var + LN_EPS) * g_ref[...] + b_ref[...]


def _combine(pos, gates_t, h1, ln_g, ln_b, ys, *, alpha):
    n_tok, d = h1.shape
    const = lambda i: (0, 0)
    n_tiles = n_tok // TB_COMB
    return pl.pallas_call(
        functools.partial(_combine_body, alpha=alpha),
        grid=(n_tiles,),
        in_specs=[
            pl.BlockSpec((TOP_K, TB_COMB), lambda i: (0, i), memory_space=pltpu.SMEM),
            pl.BlockSpec((TOP_K, TB_COMB), lambda i: (0, jnp.minimum(i + 1, n_tiles - 1)),
                         memory_space=pltpu.SMEM),
            pl.BlockSpec((TB_COMB, TOP_K), lambda i: (i, 0)),
            pl.BlockSpec((TB_COMB, d), lambda i: (i, 0)),
            pl.BlockSpec((1, d), const),
            pl.BlockSpec((1, d), const),
            pl.BlockSpec(memory_space=pl.ANY),
        ],
        out_specs=pl.BlockSpec((TB_COMB, d), lambda i: (i, 0)),
        out_shape=jax.ShapeDtypeStruct((n_tok, d), F32),
        scratch_shapes=[pltpu.VMEM((2, TOP_K, TB_COMB) + ys.shape[1:], BF16),
                        pltpu.SemaphoreType.DMA((2,))],
        compiler_params=_params(1),
        name="combine",
    )(pos, pos, gates_t, h1, ln_g.reshape(1, -1), ln_b.reshape(1, -1), ys)


def _rope_tables(seq, half):
    inv_freq = ROPE_BASE ** (-jnp.arange(half, dtype=F32) / half)
    ang = jnp.arange(seq, dtype=F32)[:, None] * inv_freq[None, :]
    return jnp.cos(ang), jnp.sin(ang)


def _layer(h, p, *, batch, seq, alpha, cos, sin):
    n_tok, d = h.shape
    conv_w = p["conv_w"].shape[1]
    ret_w = d - conv_w
    a, q, k, v, g = _in_proj(h, p["w_in"], p["b_in"], cos, sin, p["conv_w"], p["conv_b"],
                             p["conv_ln_g"], p["conv_ln_b"], seq=seq, conv_w=conv_w, ret_w=ret_w)
    h1, h1_packed, logits_t = _out_proj(a, q, k, v, g, p["w_out"], p["b_out"], h, p["ln1_g"],
                                        p["ln1_b"], p["w_router"], p["b_router"], alpha=alpha, seq=seq)
    pos, gates, meta = _route(logits_t)
    w_max = N_EXPERTS + (n_tok * TOP_K) // SLOT_ROWS
    xs = _dispatch(pos, h1_packed, w_max * SLOT_ROWS)
    ys = _experts(meta[0, :w_max], meta[1, :w_max], meta[2, :1], xs,
                  p["w_gate_up"], p["b_gate_up"], p["w_down"], p["b_down"], w_max=w_max)
    return _combine(pos, gates.T, h1, p["ln2_g"], p["ln2_b"], ys, alpha=alpha)


def kernel(x, w_in, b_in, conv_w, conv_b, conv_ln_g, conv_ln_b, w_out, b_out, ln1_g, ln1_b,
           w_router, b_router, w_gate_up, b_gate_up, w_down, b_down, ln2_g, ln2_b):
    batch, seq, d = x.shape
    depth = w_in.shape[0]
    alpha = float((2 * depth) ** 0.25)
    head_dim = (d - conv_w.shape[2]) // RET_HEADS
    cos, sin = _rope_tables(seq, head_dim // 2)
    stacked = dict(w_in=w_in, b_in=b_in, conv_w=conv_w, conv_b=conv_b, conv_ln_g=conv_ln_g,
                   conv_ln_b=conv_ln_b, w_out=w_out, b_out=b_out, ln1_g=ln1_g, ln1_b=ln1_b,
                   w_router=w_router, b_router=b_router, w_gate_up=w_gate_up, b_gate_up=b_gate_up,
                   w_down=w_down, b_down=b_down, ln2_g=ln2_g, ln2_b=ln2_b)
    h = x.reshape(batch * seq, d)
    for layer in range(depth):
        p = {name: val[layer] for name, val in stacked.items()}
        h = _layer(h, p, batch=batch, seq=seq, alpha=alpha, cos=cos, sin=sin)
    return h.reshape(batch, seq, d)
```
